```python
import jax, jax.numpy as jnp
from jax import lax
import numpy as np

D_MODEL = 1024
BATCH = 4
SEQ = 8192
DEPTH = 1
DEC_BATCH = 128
DEC_SEQ = 4
PAST_LEN = 16384
PAGE_SIZE = 128

MIX_WIDTH = D_MODEL
LRU_WIDTH = MIX_WIDTH // 2
LRU_BLOCKS = 8
LRU_BLOCK_DIM = LRU_WIDTH // LRU_BLOCKS
CONV_WIDTH = 4
LRU_C = 8.0
N_HEADS = 8
N_KV_HEADS = 2
GROUP = N_HEADS // N_KV_HEADS
HEAD_DIM = (MIX_WIDTH - LRU_WIDTH) // N_HEADS
Q_DIM = N_HEADS * HEAD_DIM
KV_DIM = N_KV_HEADS * HEAD_DIM
IN_DIM = 2 * LRU_WIDTH + Q_DIM + 2 * KV_DIM
SPLITS = (LRU_WIDTH, 2 * LRU_WIDTH, 2 * LRU_WIDTH + Q_DIM, 2 * LRU_WIDTH + Q_DIM + KV_DIM)
WINDOW = 128
BLOCK_Q = 128
ROPE_THETA = 10000.0
ATTN_SCALE = HEAD_DIM ** -0.5
D_FF = ((8 * D_MODEL + 3 * 256 - 1) // (3 * 256)) * 256
ALPHA = (2.0 * DEPTH) ** 0.25
BETA = (8.0 * DEPTH) ** -0.25

kernel_name = 'hymba_rglru_swa_sink_deepnorm_step'


def layer_norm(x, g, b, eps=1e-5):
    xf = x.astype(jnp.float32)
    mu = jnp.mean(xf, -1, keepdims=True)
    var = jnp.mean(jnp.square(xf - mu), -1, keepdims=True)
    return ((xf - mu) * lax.rsqrt(var + eps) * g.astype(jnp.float32) + b.astype(jnp.float32)).astype(x.dtype)


def rms_norm(x, g, eps=1e-6):
    xf = x.astype(jnp.float32)
    return (xf * lax.rsqrt(jnp.mean(xf * xf, -1, keepdims=True) + eps) * g.astype(jnp.float32)).astype(x.dtype)


def rope(x, positions):
    half = HEAD_DIM // 2
    inv = ROPE_THETA ** (-jnp.arange(half, dtype=jnp.float32) / half)
    ang = positions.astype(jnp.float32)[:, None] * inv[None, :]
    cos = jnp.cos(ang)[None, :, None, :]
    sin = jnp.sin(ang)[None, :, None, :]
    xf = x.astype(jnp.float32)
    x1, x2 = xf[..., :half], xf[..., half:]
    return jnp.concatenate([x1 * cos - x2 * sin, x2 * cos + x1 * sin], -1).astype(x.dtype)


def causal_conv(x_ext, conv_w, conv_b):
    s = x_ext.shape[1] - (CONV_WIDTH - 1)
    out = conv_b
    for j in range(CONV_WIDTH):
        out = out + x_ext[:, j:j + s] * conv_w[j]
    return out


def rg_lru(x, h0, w_a, b_a, w_x, b_x, lru_lambda):
    b, s, c = x.shape
    xb = x.reshape(b, s, LRU_BLOCKS, LRU_BLOCK_DIM)
    r = jax.nn.sigmoid(jnp.einsum('bshi,hij->bshj', xb, w_a) + b_a).reshape(b, s, c)
    i = jax.nn.sigmoid(jnp.einsum('bshi,hij->bshj', xb, w_x) + b_x).reshape(b, s, c)
    log_a = -LRU_C * r.astype(jnp.float32) * jax.nn.softplus(-lru_lambda.astype(jnp.float32))
    a = jnp.exp(log_a)
    u = jnp.sqrt(-jnp.expm1(2.0 * log_a)) * (i * x).astype(jnp.float32)

    def step(h, au):
        a_t, u_t = au
        h = a_t * h + u_t
        return h, h

    h_last, hs = lax.scan(step, h0.astype(jnp.float32), (a.transpose(1, 0, 2), u.transpose(1, 0, 2)))
    return hs.transpose(1, 0, 2).astype(x.dtype), h_last


def sink_softmax(scores, mask, sinks):
    s = jnp.where(mask, scores.astype(jnp.float32), -jnp.inf)
    sink = sinks.astype(jnp.float32).reshape(N_KV_HEADS, GROUP, 1, 1)
    m = jnp.maximum(jnp.max(s, -1, keepdims=True), sink)
    p = jnp.exp(s - m)
    return p / (jnp.sum(p, -1, keepdims=True) + jnp.exp(sink - m))


def swa_banded(q, k, v, sinks):
    b, s = q.shape[:2]
    nb = s // BLOCK_Q
    qb = q.reshape(b, nb, BLOCK_Q, N_KV_HEADS, GROUP, HEAD_DIM)

    def with_prev(t):
        tb = t.reshape(b, nb, BLOCK_Q, N_KV_HEADS, HEAD_DIM)
        prev = jnp.pad(tb, ((0, 0), (1, 0), (0, 0), (0, 0), (0, 0)))[:, :-1]
        return jnp.concatenate([prev, tb], axis=2)

    kk, vv = with_prev(k), with_prev(v)
    blk = jnp.arange(nb)[:, None] * BLOCK_Q
    qabs = blk + jnp.arange(BLOCK_Q)[None, :]
    kabs = blk - BLOCK_Q + jnp.arange(2 * BLOCK_Q)[None, :]
    diff = qabs[:, :, None] - kabs[:, None, :]
    mask = (diff >= 0) & (diff <= WINDOW) & (kabs[:, None, :] >= 0)
    mask = mask[:, None, None]
    scores = jnp.einsum('bnqkgd,bnskd->bnkgqs', qb, kk) * ATTN_SCALE
    p = sink_softmax(scores, mask, sinks)
    out = jnp.einsum('bnkgqs,bnskd->bnqkgd', p, vv.astype(jnp.float32))
    return out.reshape(b, s, Q_DIM).astype(q.dtype)


def swa_with_buffer(q, k, v, k_buf, v_buf, sinks, positions):
    b, t = q.shape[:2]
    w = k_buf.shape[1]
    kk = jnp.concatenate([k_buf.astype(k.dtype), k], axis=1)
    vv = jnp.concatenate([v_buf.astype(v.dtype), v], axis=1)
    kpos = jnp.concatenate([positions[0] - w + jnp.arange(w), positions])
    diff = positions[:, None] - kpos[None, :]
    mask = (diff >= 0) & (diff <= WINDOW)
    qg = q.reshape(b, t, N_KV_HEADS, GROUP, HEAD_DIM)
    scores = jnp.einsum('btkgd,bskd->bkgts', qg, kk) * ATTN_SCALE
    p = sink_softmax(scores, mask, sinks)
    out = jnp.einsum('bkgts,bskd->btkgd', p, vv.astype(jnp.float32)).reshape(b, t, Q_DIM).astype(q.dtype)
    return out, kk[:, -w:], vv[:, -w:]


def decoder_layer(x, positions, conv_hist, lru_h0, k_buf, v_buf,
                  w_in, b_in, conv_w, conv_b, w_a, b_a, w_x, b_x, lru_lambda, sinks,
                  g_lru, g_attn, w_out, b_out, ln1_g, ln1_b, w_gate, w_up, w_down, ln2_g, ln2_b):
    bsz, seq = x.shape[:2]
    z = jnp.einsum('bsd,de->bse', x, w_in) + b_in
    xr, gate, q, k, v = jnp.split(z, SPLITS, axis=-1)
    x_ext = jnp.concatenate([conv_hist.astype(xr.dtype), xr], axis=1)
    xc = causal_conv(x_ext, conv_w, conv_b)
    lru_out, lru_last = rg_lru(xc, lru_h0, w_a, b_a, w_x, b_x, lru_lambda)
    y_lru = jax.nn.gelu(gate) * lru_out
    new_conv = x_ext[:, -(CONV_WIDTH - 1):]
    q = rope(q.reshape(bsz, seq, N_HEADS, HEAD_DIM), positions)
    k = rope(k.reshape(bsz, seq, N_KV_HEADS, HEAD_DIM), positions)
    v = v.reshape(bsz, seq, N_KV_HEADS, HEAD_DIM)
    if k_buf is None:
        y_attn = swa_banded(q, k, v, sinks)
        keep = min(WINDOW, seq)
        new_k, new_v = k[:, -keep:], v[:, -keep:]
    else:
        y_attn, new_k, new_v = swa_with_buffer(q, k, v, k_buf, v_buf, sinks, positions)
    mixed = jnp.concatenate([rms_norm(y_lru, g_lru), rms_norm(y_attn, g_attn)], axis=-1)
    y_mix = jnp.einsum('bse,ed->bsd', mixed, w_out) + b_out
    h1 = layer_norm(ALPHA * x + y_mix, ln1_g, ln1_b)
    hid = jax.nn.silu(jnp.einsum('bsd,df->bsf', h1, w_gate)) * jnp.einsum('bsd,df->bsf', h1, w_up)
    ffn = jnp.einsum('bsf,fd->bsd', hid, w_down)
    h2 = layer_norm(ALPHA * h1 + ffn, ln2_g, ln2_b)
    return h2, new_conv, lru_last, new_k, new_v


def setup_inputs(seed: int = 0) -> dict:
    key = jax.random.key(seed)
    ks = jax.random.split(key, 32)
    nrm = jax.random.normal
    win = min(WINDOW, PAST_LEN)
    x_prompt = nrm(ks[0], (BATCH, SEQ, D_MODEL), jnp.float32)
    x_sample = nrm(ks[1], (DEC_BATCH, DEC_SEQ, D_MODEL), jnp.float32)
    cache_k_win = nrm(ks[2], (DEPTH, DEC_BATCH, win, N_KV_HEADS, HEAD_DIM), jnp.float32)
    cache_v_win = nrm(ks[3], (DEPTH, DEC_BATCH, win, N_KV_HEADS, HEAD_DIM), jnp.float32)
    state_conv = nrm(ks[4], (DEPTH, DEC_BATCH, CONV_WIDTH - 1, LRU_WIDTH), jnp.float32)
    state_lru = 0.5 * nrm(ks[5], (DEPTH, DEC_BATCH, LRU_WIDTH), jnp.float32)
    w_in = nrm(ks[6], (DEPTH, D_MODEL, IN_DIM), jnp.float32) * D_MODEL ** -0.5
    v_scale = jnp.concatenate([jnp.ones((IN_DIM - KV_DIM,), jnp.float32), jnp.full((KV_DIM,), BETA, jnp.float32)])
    w_in = w_in * v_scale
    b_in = 0.01 * nrm(ks[7], (DEPTH, IN_DIM), jnp.float32)
    conv_w = nrm(ks[8], (DEPTH, CONV_WIDTH, LRU_WIDTH), jnp.float32) * CONV_WIDTH ** -0.5
    conv_b = 0.01 * nrm(ks[9], (DEPTH, LRU_WIDTH), jnp.float32)
    w_a = nrm(ks[10], (DEPTH, LRU_BLOCKS, LRU_BLOCK_DIM, LRU_BLOCK_DIM), jnp.float32) * LRU_BLOCK_DIM ** -0.5
    b_a = 0.01 * nrm(ks[11], (DEPTH, LRU_BLOCKS, LRU_BLOCK_DIM), jnp.float32)
    w_x = nrm(ks[12], (DEPTH, LRU_BLOCKS, LRU_BLOCK_DIM, LRU_BLOCK_DIM), jnp.float32) * LRU_BLOCK_DIM ** -0.5
    b_x = 0.01 * nrm(ks[13], (DEPTH, LRU_BLOCKS, LRU_BLOCK_DIM), jnp.float32)
    a_init = jax.random.uniform(ks[14], (DEPTH, LRU_WIDTH), jnp.float32, minval=0.9, maxval=0.999)
    sig = a_init ** (1.0 / LRU_C)
    lru_lambda = jnp.log(sig) - jnp.log1p(-sig)
    sinks = 0.5 * nrm(ks[15], (DEPTH, N_HEADS), jnp.float32)
    g_lru = 1.0 + 0.02 * nrm(ks[16], (DEPTH, LRU_WIDTH), jnp.float32)
    g_attn = 1.0 + 0.02 * nrm(ks[17], (DEPTH, Q_DIM), jnp.float32)
    w_out = nrm(ks[18], (DEPTH, MIX_WIDTH, D_MODEL), jnp.float32) * MIX_WIDTH ** -0.5 * BETA
    b_out = 0.01 * nrm(ks[19], (DEPTH, D_MODEL), jnp.float32)
    ln1_g = 1.0 + 0.02 * nrm(ks[20], (DEPTH, D_MODEL), jnp.float32)
    ln1_b = 0.01 * nrm(ks[21], (DEPTH, D_MODEL), jnp.float32)
    w_gate = nrm(ks[22], (DEPTH, D_MODEL, D_FF), jnp.float32) * D_MODEL ** -0.5 * BETA
    w_up = nrm(ks[23], (DEPTH, D_MODEL, D_FF), jnp.float32) * D_MODEL ** -0.5 * BETA
    w_down = nrm(ks[24], (DEPTH, D_FF, D_MODEL), jnp.float32) * D_FF ** -0.5 * BETA
    ln2_g = 1.0 + 0.02 * nrm(ks[25], (DEPTH, D_MODEL), jnp.float32)
    ln2_b = 0.01 * nrm(ks[26], (DEPTH, D_MODEL), jnp.float32)
    return {'x_prompt': x_prompt, 'x_sample': x_sample,
            'cache_k_win': cache_k_win, 'cache_v_win': cache_v_win,
            'state_conv': state_conv, 'state_lru': state_lru,
            'w_in': w_in, 'b_in': b_in, 'conv_w': conv_w, 'conv_b': conv_b,
            'w_a': w_a, 'b_a': b_a, 'w_x': w_x, 'b_x': b_x, 'lru_lambda': lru_lambda,
            'sinks': sinks, 'g_lru': g_lru, 'g_attn': g_attn, 'w_out': w_out, 'b_out': b_out,
            'ln1_g': ln1_g, 'ln1_b': ln1_b, 'w_gate': w_gate, 'w_up': w_up, 'w_down': w_down,
            'ln2_g': ln2_g, 'ln2_b': ln2_b}


def reference(x_prompt, x_sample, cache_k_win, cache_v_win, state_conv, state_lru,
              w_in, b_in, conv_w, conv_b, w_a, b_a, w_x, b_x, lru_lambda,
              sinks, g_lru, g_attn, w_out, b_out, ln1_g, ln1_b, w_gate, w_up, w_down, ln2_g, ln2_b):
    pos_p = jnp.arange(x_prompt.shape[1], dtype=jnp.int32)
    pos_s = PAST_LEN + jnp.arange(x_sample.shape[1], dtype=jnp.int32)
    bp = x_prompt.shape[0]
    hp, hs = x_prompt, x_sample
    conv_p, lru_p, kp_l, vp_l = [], [], [], []
    conv_s, lru_s, ks_l, vs_l = [], [], [], []
    for l in range(DEPTH):
        p = (w_in[l], b_in[l], conv_w[l], conv_b[l], w_a[l], b_a[l], w_x[l], b_x[l], lru_lambda[l], sinks[l],
             g_lru[l], g_attn[l], w_out[l], b_out[l], ln1_g[l], ln1_b[l], w_gate[l], w_up[l], w_down[l],
             ln2_g[l], ln2_b[l])
        zero_conv = jnp.zeros((bp, CONV_WIDTH - 1, LRU_WIDTH), x_prompt.dtype)
        zero_h = jnp.zeros((bp, LRU_WIDTH), jnp.float32)
        hp, c1, h1, k1, v1 = decoder_layer(hp, pos_p, zero_conv, zero_h, None, None, *p)
        hs, c2, h2, k2, v2 = decoder_layer(hs, pos_s, state_conv[l], state_lru[l], cache_k_win[l], cache_v_win[l], *p)
        conv_p.append(c1); lru_p.append(h1); kp_l.append(k1); vp_l.append(v1)
        conv_s.append(c2); lru_s.append(h2); ks_l.append(k2); vs_l.append(v2)
    return (hp, hs,
            jnp.stack(conv_p), jnp.stack(lru_p), jnp.stack(kp_l), jnp.stack(vp_l),
            jnp.stack(conv_s), jnp.stack(lru_s), jnp.stack(ks_l), jnp.stack(vs_l))
```

```python
import functools

import jax
import jax.numpy as jnp
from jax import lax
from jax.experimental import pallas as pl
from jax.experimental.pallas import tpu as pltpu

D_MODEL = 1024
LRU_WIDTH = 512
LRU_BLOCKS = 8
CONV_WIDTH = 4
LRU_C = 8.0
N_HEADS = 8
N_KV_HEADS = 2
HEAD_DIM = 64
HALF_DIM = HEAD_DIM // 2
Q_DIM = N_HEADS * HEAD_DIM
KV_DIM = N_KV_HEADS * HEAD_DIM
IN_DIM = 2 * LRU_WIDTH + Q_DIM + 2 * KV_DIM
WINDOW = 128
BLOCK_Q = 128
ROPE_THETA = 10000.0
ATTN_SCALE = HEAD_DIM ** -0.5
D_FF = 2816
ALPHA = 2.0 ** 0.25
PAST_LEN = 16384

LANES = 128
SUBLANES = 8
VMEM_LIMIT_BYTES = 56 * 1024 * 1024

_XR0, _GATE0, _Q0, _K0, _V0 = 0, LRU_WIDTH, 2 * LRU_WIDTH, 2 * LRU_WIDTH + Q_DIM, 2 * LRU_WIDTH + Q_DIM + KV_DIM

PROMPT_TILE = 256
FFN_TILE = 512
FFN_CHUNK = 256
DEC_BLOCK = 32
DEC_SUB = 8

BF16 = jnp.bfloat16
F32 = jnp.float32


def _dot(a, b):
    return jnp.dot(a, b, preferred_element_type=F32)


def _dot_nt(a, b):
    return lax.dot_general(a, b, (((1,), (1,)), ((), ())), preferred_element_type=F32)


def _layer_norm(x, g, b, eps=1e-5):
    mu = jnp.mean(x, axis=-1, keepdims=True)
    xc = x - mu
    var = jnp.mean(xc * xc, axis=-1, keepdims=True)
    return xc * lax.rsqrt(var + eps) * g + b


def _rms_norm(x, g, eps=1e-6):
    return x * lax.rsqrt(jnp.mean(x * x, axis=-1, keepdims=True) + eps) * g


def _gelu_tanh(x):
    c = 0.7978845608028654
    return 0.5 * x * (1.0 + jnp.tanh(c * (x + 0.044715 * (x * x * x))))


def _softplus(x):
    return jnp.maximum(x, 0.0) + jnp.log1p(jnp.exp(-jnp.abs(x)))


def _swap_halves(x):
    w = x.shape[-1]
    lane = lax.broadcasted_iota(jnp.int32, x.shape, 1)
    first_half = (lane % HEAD_DIM) < HALF_DIM
    return jnp.where(first_half, pltpu.roll(x, w - HALF_DIM, 1), pltpu.roll(x, HALF_DIM, 1))


def _rope(x, cos, sin_signed):
    reps = x.shape[-1] // LANES
    if reps > 1:
        cos = jnp.concatenate([cos] * reps, axis=1)
        sin_signed = jnp.concatenate([sin_signed] * reps, axis=1)
    return x * cos + _swap_halves(x) * sin_signed


def _gates(xc, wa_ref, wx_ref, b_a, b_x, lam):
    xcb = xc.astype(BF16)
    half = LRU_WIDTH // 2
    lo, hi = xcb[:, :half], xcb[:, half:]
    ra = jnp.concatenate([_dot(lo, wa_ref[0]), _dot(hi, wa_ref[1])], axis=1) + b_a
    ia = jnp.concatenate([_dot(lo, wx_ref[0]), _dot(hi, wx_ref[1])], axis=1) + b_x
    r = jax.nn.sigmoid(ra)
    i = jax.nn.sigmoid(ia)
    log_a = (-LRU_C * _softplus(-lam)) * r
    a = jnp.exp(log_a)
    mult = jnp.sqrt(-jnp.tanh(log_a) * (a * a + 1.0))
    u = mult * (i * xc)
    return a, u


def _lane_halves(x, kvh):
    lane = lax.broadcasted_iota(jnp.int32, x.shape, 1)
    low = lane < HEAD_DIM
    rolled = pltpu.roll(x, HEAD_DIM, 1)
    if kvh == 0:
        return jnp.where(low, x, 0.0), jnp.where(low, 0.0, rolled)
    return jnp.where(low, rolled, 0.0), jnp.where(low, 0.0, x)


def _sink_attention_pair(q_pair, kk, vv, mask, sink_lo, sink_hi):
    s_len = mask.shape[1]
    scores = _dot_nt(q_pair, kk)
    ps, invs = [], []
    for half, sink in ((0, sink_lo), (1, sink_hi)):
        sh = jnp.where(mask, scores[:, half * s_len:(half + 1) * s_len], -jnp.inf)
        m = jnp.maximum(jnp.max(sh, axis=-1, keepdims=True), sink)
        p = jnp.exp(sh - m)
        denom = jnp.sum(p, axis=-1, keepdims=True) + jnp.exp(sink - m)
        ps.append(p.astype(BF16))
        invs.append(1.0 / denom)
    out = _dot(jnp.concatenate(ps, axis=1), vv)
    lane = lax.broadcasted_iota(jnp.int32, out.shape, 1)
    return out * jnp.where(lane < HEAD_DIM, invs[0], invs[1])


def _prompt_mixer_kernel(x_ref, cos_ref, sin_ref, w_in_ref, b_in_ref, conv_w_ref, conv_b_ref,
                         wa_ref, wx_ref, b_a_ref, b_x_ref, lam_ref, sinks_ref, g_lru_ref, g_attn_ref,
                         w_out_ref, b_out_ref, ln_g_ref, ln_b_ref,
                         h1_ref, conv_out_ref, lru_out_ref, k_out_ref, v_out_ref,
                         xr_ext, h_carry, k_prev, v_prev):
    t_idx = pl.program_id(1)
    n_t = pl.num_programs(1)
    tile = x_ref.shape[0]
    hist = SUBLANES

    @pl.when(t_idx == 0)
    def _():
        xr_ext[0:hist, :] = jnp.zeros((hist, LRU_WIDTH), F32)
        h_carry[...] = jnp.zeros_like(h_carry)
        k_prev[...] = jnp.zeros_like(k_prev)
        v_prev[...] = jnp.zeros_like(v_prev)

    x = x_ref[...]
    xb = x.astype(BF16)

    def proj(c0, width):
        return _dot(xb, w_in_ref[:, c0:c0 + width]) + b_in_ref[:, c0:c0 + width]

    xr = proj(_XR0, LRU_WIDTH)
    xr_ext[hist:hist + tile, :] = xr
    xc = conv_b_ref[...] + xr * conv_w_ref[CONV_WIDTH - 1:CONV_WIDTH, :]
    for j in range(CONV_WIDTH - 1):
        shift = CONV_WIDTH - 1 - j
        xc = xc + xr_ext[hist - shift:hist - shift + tile, :] * conv_w_ref[j:j + 1, :]
    xr_ext[0:hist, :] = xr_ext[tile:tile + hist, :]

    a, u = _gates(xc, wa_ref, wx_ref, b_a_ref[...], b_x_ref[...], lam_ref[...])

    groups = tile // SUBLANES
    a3 = a.reshape(groups, SUBLANES, LRU_WIDTH)
    u3 = u.reshape(groups, SUBLANES, LRU_WIDTH)
    row = lax.broadcasted_iota(jnp.int32, a3.shape, 1)
    d = 1
    while d < SUBLANES:
        valid = row >= d
        u3 = u3 + a3 * jnp.where(valid, pltpu.roll(u3, d, 1), 0.0)
        a3 = a3 * jnp.where(valid, pltpu.roll(a3, d, 1), 1.0)
        d *= 2
    h = h_carry[...]
    hs = []
    for g in range(groups):
        hg = u3[g] + a3[g] * h
        h = hg[SUBLANES - 1:SUBLANES, :]
        hs.append(hg)
    h_carry[...] = h
    h_all = jnp.concatenate(hs, axis=0)

    gate = proj(_GATE0, LRU_WIDTH)
    y_lru = _rms_norm(_gelu_tanh(gate) * h_all, g_lru_ref[...])

    cos = cos_ref[...]
    sin = sin_ref[...]
    q = _rope(proj(_Q0, Q_DIM), cos, sin) * ATTN_SCALE
    k = _rope(proj(_K0, KV_DIM), cos, sin)
    v = proj(_V0, KV_DIM)
    qb = q.astype(BF16)

    qi = lax.broadcasted_iota(jnp.int32, (BLOCK_Q, 2 * BLOCK_Q), 0)
    kj = lax.broadcasted_iota(jnp.int32, (BLOCK_Q, 2 * BLOCK_Q), 1)
    band = (kj >= qi) & (kj <= qi + WINDOW)
    first_lo = jnp.where(t_idx == 0, BLOCK_Q, 0)
    band_first = band & (kj >= first_lo)

    n_blocks = tile // BLOCK_Q
    attn_rows = []
    for n in range(n_blocks):
        rows = slice(n * BLOCK_Q, (n + 1) * BLOCK_Q)
        if n == 0:
            k2 = jnp.concatenate([k_prev[...], k[rows]], axis=0)
            v2 = jnp.concatenate([v_prev[...], v[rows]], axis=0)
            mask = band_first
        else:
            k2 = k[(n - 1) * BLOCK_Q:(n + 1) * BLOCK_Q]
            v2 = v[(n - 1) * BLOCK_Q:(n + 1) * BLOCK_Q]
            mask = band
        tiles = []
        for kvh in range(N_KV_HEADS):
            kk = jnp.concatenate(_lane_halves(k2, kvh), axis=0).astype(BF16)
            vv = jnp.concatenate(_lane_halves(v2, kvh), axis=0).astype(BF16)
            for cc in range(2):
                c = 2 * kvh + cc
                tiles.append(_sink_attention_pair(qb[rows, c * LANES:(c + 1) * LANES], kk, vv, mask,
                                                  sinks_ref[2 * c], sinks_ref[2 * c + 1]))
        attn_rows.append(jnp.concatenate(tiles, axis=1))
    y_attn = _rms_norm(jnp.concatenate(attn_rows, axis=0), g_attn_ref[...])

    k_prev[...] = k[tile - BLOCK_Q:tile]
    v_prev[...] = v[tile - BLOCK_Q:tile]

    y_mix = (_dot(y_lru.astype(BF16), w_out_ref[0:LRU_WIDTH, :])
             + _dot(y_attn.astype(BF16), w_out_ref[LRU_WIDTH:LRU_WIDTH + Q_DIM, :]) + b_out_ref[...])
    h1_ref[...] = _layer_norm(ALPHA * x + y_mix, ln_g_ref[...], ln_b_ref[...])

    @pl.when(t_idx == n_t - 1)
    def _():
        conv_out_ref[...] = xr_ext[hist - (CONV_WIDTH - 1):hist, :]
        lru_out_ref[...] = h
        k_out_ref[...] = k[tile - WINDOW:tile]
        v_out_ref[...] = v[tile - WINDOW:tile]


def _full_spec(shape):
    zeros = (0,) * len(shape)
    return pl.BlockSpec(shape, lambda *_: zeros, pipeline_mode=pl.Buffered(1))


def _mixer_weight_specs():
    return [
        _full_spec((D_MODEL, IN_DIM)), _full_spec((1, IN_DIM)),
        _full_spec((CONV_WIDTH, LRU_WIDTH)), _full_spec((1, LRU_WIDTH)),
        _full_spec((2, LRU_WIDTH // 2, LRU_WIDTH // 2)), _full_spec((2, LRU_WIDTH // 2, LRU_WIDTH // 2)),
        _full_spec((1, LRU_WIDTH)), _full_spec((1, LRU_WIDTH)), _full_spec((1, LRU_WIDTH)),
        pl.BlockSpec(memory_space=pltpu.SMEM),
        _full_spec((1, LRU_WIDTH)), _full_spec((1, Q_DIM)),
        _full_spec((LRU_WIDTH + Q_DIM, D_MODEL)), _full_spec((1, D_MODEL)),
        _full_spec((1, D_MODEL)), _full_spec((1, D_MODEL)),
    ]


def _prompt_mixer(x, cos, sin, weights):
    bsz, seq, _ = x.shape
    tile = PROMPT_TILE
    assert seq % tile == 0 and tile % BLOCK_Q == 0 and tile >= WINDOW
    n_t = seq // tile
    keep = min(WINDOW, seq)
    in_specs = [
        pl.BlockSpec((None, tile, D_MODEL), lambda b, t: (b, t, 0)),
        pl.BlockSpec((tile, LANES), lambda b, t: (t, 0)),
        pl.BlockSpec((tile, LANES), lambda b, t: (t, 0)),
    ] + _mixer_weight_specs()
    out_shape = (
        jax.ShapeDtypeStruct((bsz, seq, D_MODEL), F32),
        jax.ShapeDtypeStruct((bsz, CONV_WIDTH - 1, LRU_WIDTH), F32),
        jax.ShapeDtypeStruct((bsz, 1, LRU_WIDTH), F32),
        jax.ShapeDtypeStruct((bsz, keep, KV_DIM), F32),
        jax.ShapeDtypeStruct((bsz, keep, KV_DIM), F32),
    )
    out_specs = (
        pl.BlockSpec((None, tile, D_MODEL), lambda b, t: (b, t, 0)),
        pl.BlockSpec((None, CONV_WIDTH - 1, LRU_WIDTH), lambda b, t: (b, 0, 0)),
        pl.BlockSpec((None, 1, LRU_WIDTH), lambda b, t: (b, 0, 0)),
        pl.BlockSpec((None, keep, KV_DIM), lambda b, t: (b, 0, 0)),
        pl.BlockSpec((None, keep, KV_DIM), lambda b, t: (b, 0, 0)),
    )
    scratch = [
        pltpu.VMEM((tile + SUBLANES, LRU_WIDTH), F32),
        pltpu.VMEM((1, LRU_WIDTH), F32),
        pltpu.VMEM((BLOCK_Q, KV_DIM), F32),
        pltpu.VMEM((BLOCK_Q, KV_DIM), F32),
    ]
    return pl.pallas_call(
        _prompt_mixer_kernel,
        grid=(bsz, n_t),
        in_specs=in_specs,
        out_specs=out_specs,
        out_shape=out_shape,
        scratch_shapes=scratch,
        compiler_params=pltpu.CompilerParams(
            dimension_semantics=("arbitrary", "arbitrary"), vmem_limit_bytes=VMEM_LIMIT_BYTES),
        name="prompt_mixer",
    )(x, cos, sin, *weights)


def _ffn_kernel(h_ref, wg_ref, wu_ref, wd_ref, ln_g_ref, ln_b_ref, o_ref):
    slabs = h_ref.shape[1] // D_MODEL
    h = jnp.concatenate([h_ref[:, s * D_MODEL:(s + 1) * D_MODEL] for s in range(slabs)], axis=0)
    hb = h.astype(BF16)
    acc = jnp.zeros(h.shape, F32)
    for c0 in range(0, D_FF, FFN_CHUNK):
        g = _dot(hb, wg_ref[:, c0:c0 + FFN_CHUNK])
        up = _dot(hb, wu_ref[:, c0:c0 + FFN_CHUNK])
        hid = (g * jax.nn.sigmoid(g) * up).astype(BF16)
        acc = acc + _dot(hid, wd_ref[c0:c0 + FFN_CHUNK, :])
    out = _layer_norm(ALPHA * h + acc, ln_g_ref[...], ln_b_ref[...])
    rows = h_ref.shape[0]
    for s in range(slabs):
        o_ref[:, s * D_MODEL:(s + 1) * D_MODEL] = out[s * rows:(s + 1) * rows]


def _ffn(h, wg, wu, wd, ln_g, ln_b, tile):
    rows, width = h.shape
    assert rows % tile == 0 and width % D_MODEL == 0 and D_FF % FFN_CHUNK == 0
    return pl.pallas_call(
        _ffn_kernel,
        grid=(rows // tile,),
        in_specs=[
            pl.BlockSpec((tile, width), lambda i: (i, 0)),
            _full_spec((D_MODEL, D_FF)), _full_spec((D_MODEL, D_FF)), _full_spec((D_FF, D_MODEL)),
            _full_spec((1, D_MODEL)), _full_spec((1, D_MODEL)),
        ],
        out_specs=pl.BlockSpec((tile, width), lambda i: (i, 0)),
        out_shape=jax.ShapeDtypeStruct((rows, width), F32),
        compiler_params=pltpu.CompilerParams(
            dimension_semantics=("arbitrary",), vmem_limit_bytes=VMEM_LIMIT_BYTES),
        name="ffn",
    )(h, wg, wu, wd, ln_g, ln_b)


def _decode_mixer_kernel(x_ref, kc_ref, vc_ref, sconv_ref, slru_ref, cos_ref, sin_ref,
                         w_in_ref, b_in_ref, conv_w_ref, conv_b_ref,
                         wa_ref, wx_ref, b_a_ref, b_x_ref, lam_ref, sinks_ref, g_lru_ref, g_attn_ref,
                         w_out_ref, b_out_ref, ln_g_ref, ln_b_ref,
                         h1_ref, conv_out_ref, lru_out_ref, k_out_ref, v_out_ref):
    nb = x_ref.shape[0]
    steps = x_ref.shape[1] // D_MODEL
    win = kc_ref.shape[1]

    x = jnp.concatenate([x_ref[:, t * D_MODEL:(t + 1) * D_MODEL] for t in range(steps)], axis=0)
    xb = x.astype(BF16)

    def proj(c0, width):
        return _dot(xb, w_in_ref[:, c0:c0 + width]) + b_in_ref[:, c0:c0 + width]

    def step_rows(arr, t):
        return arr[t * nb:(t + 1) * nb]

    xr = proj(_XR0, LRU_WIDTH)
    ext = [sconv_ref[:, j * LRU_WIDTH:(j + 1) * LRU_WIDTH] for j in range(CONV_WIDTH - 1)]
    ext += [step_rows(xr, t) for t in range(steps)]
    xc_steps = []
    for t in range(steps):
        acc = conv_b_ref[...]
        for j in range(CONV_WIDTH):
            acc = acc + ext[t + j] * conv_w_ref[j:j + 1, :]
        xc_steps.append(acc)
    for j in range(CONV_WIDTH - 1):
        conv_out_ref[:, j * LRU_WIDTH:(j + 1) * LRU_WIDTH] = ext[steps + j]
    xc = jnp.concatenate(xc_steps, axis=0)

    a, u = _gates(xc, wa_ref, wx_ref, b_a_ref[...], b_x_ref[...], lam_ref[...])
    h = slru_ref[...]
    hs = []
    for t in range(steps):
        h = step_rows(a, t) * h + step_rows(u, t)
        hs.append(h)
    lru_out_ref[...] = h
    gate = proj(_GATE0, LRU_WIDTH)
    y_lru = _rms_norm(_gelu_tanh(gate) * jnp.concatenate(hs, axis=0), g_lru_ref[...])

    def table(ref):
        return jnp.concatenate([jnp.broadcast_to(ref[t:t + 1, :], (nb, LANES)) for t in range(steps)], axis=0)

    cos = table(cos_ref)
    sin = table(sin_ref)
    q = _rope(proj(_Q0, Q_DIM), cos, sin) * ATTN_SCALE
    k = _rope(proj(_K0, KV_DIM), cos, sin)
    v = proj(_V0, KV_DIM)

    k_out_ref[:, 0:win - steps, :] = kc_ref[:, steps:win, :]
    v_out_ref[:, 0:win - steps, :] = vc_ref[:, steps:win, :]
    for t in range(steps):
        k_out_ref[:, win - steps + t, :] = step_rows(k, t)
        v_out_ref[:, win - steps + t, :] = step_rows(v, t)

    sub = DEC_SUB
    n_sub = nb // sub
    n_cache = sub * win
    n_keys = n_cache + steps * sub
    qr = lax.broadcasted_iota(jnp.int32, (steps * sub, n_keys), 0)
    kc_col = lax.broadcasted_iota(jnp.int32, (steps * sub, n_keys), 1)
    q_step, q_seq = qr // sub, qr % sub
    cached = kc_col < n_cache
    new_col = kc_col - n_cache
    key_seq = jnp.where(cached, kc_col // win, new_col % sub)
    key_lo = jnp.where(cached, q_step + (win - WINDOW), 0)
    key_hi = jnp.where(cached, win - 1, q_step)
    key_idx = jnp.where(cached, kc_col % win, new_col // sub)
    mask = (key_seq == q_seq) & (key_idx >= key_lo) & (key_idx <= key_hi)

    def sub_rows(arr, j):
        return jnp.concatenate([arr[t * nb + j * sub:t * nb + (j + 1) * sub] for t in range(steps)], axis=0)

    attn_sub = []
    for j in range(n_sub):
        qj = sub_rows(q, j).astype(BF16)
        kall = jnp.concatenate([kc_ref[j * sub:(j + 1) * sub].reshape(n_cache, KV_DIM), sub_rows(k, j)], axis=0)
        vall = jnp.concatenate([vc_ref[j * sub:(j + 1) * sub].reshape(n_cache, KV_DIM), sub_rows(v, j)], axis=0)
        tiles = []
        for kvh in range(N_KV_HEADS):
            kk = jnp.concatenate(_lane_halves(kall, kvh), axis=0).astype(BF16)
            vv = jnp.concatenate(_lane_halves(vall, kvh), axis=0).astype(BF16)
            for cc in range(2):
                c = 2 * kvh + cc
                tiles.append(_sink_attention_pair(qj[:, c * LANES:(c + 1) * LANES], kk, vv, mask,
                                                  sinks_ref[2 * c], sinks_ref[2 * c + 1]))
        attn_sub.append(jnp.concatenate(tiles, axis=1))
    y_attn = jnp.concatenate(
        [attn_sub[j][t * sub:(t + 1) * sub] for t in range(steps) for j in range(n_sub)], axis=0)
    y_attn = _rms_norm(y_attn, g_attn_ref[...])

    y_mix = (_dot(y_lru.astype(BF16), w_out_ref[0:LRU_WIDTH, :])
             + _dot(y_attn.astype(BF16), w_out_ref[LRU_WIDTH:LRU_WIDTH + Q_DIM, :]) + b_out_ref[...])
    h1 = _layer_norm(ALPHA * x + y_mix, ln_g_ref[...], ln_b_ref[...])
    for t in range(steps):
        h1_ref[:, t * D_MODEL:(t + 1) * D_MODEL] = step_rows(h1, t)


def _decode_mixer(x2d, kc, vc, sconv, slru, cos, sin, weights):
    nseq, width = x2d.shape
    steps = width // D_MODEL
    win = kc.shape[1]
    nb = DEC_BLOCK
    assert nseq % nb == 0 and nb % DEC_SUB == 0 and DEC_SUB % SUBLANES == 0 and steps <= win
    conv_w = (CONV_WIDTH - 1) * LRU_WIDTH
    in_specs = [
        pl.BlockSpec((nb, width), lambda i: (i, 0)),
        pl.BlockSpec((nb, win, KV_DIM), lambda i: (i, 0, 0)),
        pl.BlockSpec((nb, win, KV_DIM), lambda i: (i, 0, 0)),
        pl.BlockSpec((nb, conv_w), lambda i: (i, 0)),
        pl.BlockSpec((nb, LRU_WIDTH), lambda i: (i, 0)),
        _full_spec((steps, LANES)), _full_spec((steps, LANES)),
    ] + _mixer_weight_specs()
    out_shape = (
        jax.ShapeDtypeStruct((nseq, width), F32),
        jax.ShapeDtypeStruct((nseq, conv_w), F32),
        jax.ShapeDtypeStruct((nseq, LRU_WIDTH), F32),
        jax.ShapeDtypeStruct((nseq, win, KV_DIM), F32),
        jax.ShapeDtypeStruct((nseq, win, KV_DIM), F32),
    )
    out_specs = (
        pl.BlockSpec((nb, width), lambda i: (i, 0)),
        pl.BlockSpec((nb, conv_w), lambda i: (i, 0)),
        pl.BlockSpec((nb, LRU_WIDTH), lambda i: (i, 0)),
        pl.BlockSpec((nb, win, KV_DIM), lambda i: (i, 0, 0)),
        pl.BlockSpec((nb, win, KV_DIM), lambda i: (i, 0, 0)),
    )
    return pl.pallas_call(
        _decode_mixer_kernel,
        grid=(nseq // nb,),
        in_specs=in_specs,
        out_specs=out_specs,
        out_shape=out_shape,
        compiler_params=pltpu.CompilerParams(
            dimension_semantics=("arbitrary",), vmem_limit_bytes=VMEM_LIMIT_BYTES),
        name="decode_mixer",
    )(x2d, kc, vc, sconv, slru, cos, sin, *weights)


def _rope_tables(positions):
    inv = ROPE_THETA ** (-jnp.arange(HALF_DIM, dtype=F32) / HALF_DIM)
    ang = positions.astype(F32)[:, None] * inv[None, :]
    cos = jnp.cos(ang)
    sin = jnp.sin(ang)
    reps = LANES // HEAD_DIM
    cos_t = jnp.tile(jnp.concatenate([cos, cos], axis=1), (1, reps))
    sin_t = jnp.tile(jnp.concatenate([-sin, sin], axis=1), (1, reps))
    return cos_t, sin_t


def _block_diag_halves(w):
    per = LRU_BLOCKS // 2
    d = w.shape[-1]
    w4 = w.reshape(2, per, d, d)
    eye = jnp.eye(per, dtype=w.dtype)
    return jnp.einsum('paij,ab->paibj', w4, eye).reshape(2, per * d, per * d)


def kernel(x_prompt, x_sample, cache_k_win, cache_v_win, state_conv, state_lru, w_in, b_in, conv_w, conv_b,
           w_a, b_a, w_x, b_x, lru_lambda, sinks, g_lru, g_attn, w_out, b_out, ln1_g, ln1_b,
           w_gate, w_up, w_down, ln2_g, ln2_b):
    depth = w_in.shape[0]
    assert depth == 1
    l = 0
    bsz, seq, _ = x_prompt.shape
    nseq, steps, _ = x_sample.shape
    win = cache_k_win.shape[2]

    def row(p):
        return p[l].reshape(1, -1)

    mixer_w = (
        w_in[l].astype(BF16), row(b_in), conv_w[l], row(conv_b),
        _block_diag_halves(w_a[l]).astype(BF16), _block_diag_halves(w_x[l]).astype(BF16),
        row(b_a), row(b_x), row(lru_lambda), sinks[l],
        row(g_lru), row(g_attn), w_out[l].astype(BF16), row(b_out), row(ln1_g), row(ln1_b),
    )
    ffn_w = (w_gate[l].astype(BF16), w_up[l].astype(BF16), w_down[l].astype(BF16), row(ln2_g), row(ln2_b))

    cos_p, sin_p = _rope_tables(jnp.arange(seq, dtype=jnp.int32))
    h1_p, conv_p, lru_p, k_p, v_p = _prompt_mixer(x_prompt, cos_p, sin_p, mixer_w)
    y_p = _ffn(h1_p.reshape(bsz * seq, D_MODEL), *ffn_w, tile=FFN_TILE).reshape(bsz, seq, D_MODEL)

    cos_s, sin_s = _rope_tables(PAST_LEN + jnp.arange(steps, dtype=jnp.int32))
    h1_s, conv_s, lru_s, k_s, v_s = _decode_mixer(
        x_sample.reshape(nseq, steps * D_MODEL),
        cache_k_win[l].reshape(nseq, win, KV_DIM), cache_v_win[l].reshape(nseq, win, KV_DIM),
        state_conv[l].reshape(nseq, (CONV_WIDTH - 1) * LRU_WIDTH), state_lru[l],
        cos_s, sin_s, mixer_w)
    y_s = _ffn(h1_s, *ffn_w, tile=nseq).reshape(nseq, steps, D_MODEL)

    keep = k_p.shape[1]
    return (
        y_p, y_s,
        conv_p[None], lru_p.reshape(1, bsz, LRU_WIDTH),
        k_p.reshape(1, bsz, keep, N_KV_HEADS, HEAD_DIM), v_p.reshape(1, bsz, keep, N_KV_HEADS, HEAD_DIM),
        conv_s.reshape(1, nseq, CONV_WIDTH - 1, LRU_WIDTH), lru_s[None],
        k_s.reshape(1, nseq, win, N_KV_HEADS, HEAD_DIM), v_s.reshape(1, nseq, win, N_KV_HEADS, HEAD_DIM),
    )
```

```python
import jax
import jax.numpy as jnp
from jax import lax
from jax.experimental import pallas as pl
from jax.experimental.pallas import tpu as pltpu

D_MODEL = 1024
LRU_WIDTH = 512
LRU_BLOCKS = 8
CONV_WIDTH = 4
LRU_C = 8.0
N_HEADS = 8
N_KV_HEADS = 2
HEAD_DIM = 64
HALF_DIM = HEAD_DIM // 2
Q_DIM = N_HEADS * HEAD_DIM
KV_DIM = N_KV_HEADS * HEAD_DIM
IN_DIM = 2 * LRU_WIDTH + Q_DIM + 2 * KV_DIM
WINDOW = 128
BLOCK_Q = 128
ROPE_THETA = 10000.0
ATTN_SCALE = HEAD_DIM ** -0.5
D_FF = 2816
ALPHA = 2.0 ** 0.25
PAST_LEN = 16384
LOG2E = 1.4426950408889634

LANES = 128
SUBLANES = 8
VMEM_LIMIT_BYTES = 56 * 1024 * 1024

_XR0, _GATE0, _Q0, _K0, _V0 = 0, LRU_WIDTH, 2 * LRU_WIDTH, 2 * LRU_WIDTH + Q_DIM, 2 * LRU_WIDTH + Q_DIM + KV_DIM

PROMPT_TILE = 256
FFN_TILE = 512
FFN_CHUNK = 256
DEC_BLOCK = 32
DEC_SUB = 8

BF16 = jnp.bfloat16
F32 = jnp.float32


def _dot(a, b):
    return jnp.dot(a, b, preferred_element_type=F32)


def _dot_nt(a, b):
    return lax.dot_general(a, b, (((1,), (1,)), ((), ())), preferred_element_type=F32)


def _layer_norm(x, g, b, eps=1e-5):
    mu = jnp.mean(x, axis=-1, keepdims=True)
    xc = x - mu
    var = jnp.mean(xc * xc, axis=-1, keepdims=True)
    return xc * lax.rsqrt(var + eps) * g + b


def _rms_norm(x, g, eps=1e-6):
    return x * lax.rsqrt(jnp.mean(x * x, axis=-1, keepdims=True) + eps) * g


def _gelu_tanh(x):
    c = 0.7978845608028654
    return 0.5 * x * (1.0 + jnp.tanh(c * (x + 0.044715 * (x * x * x))))


def _softplus(x):
    return jnp.maximum(x, 0.0) + jnp.log1p(jnp.exp(-jnp.abs(x)))


def _swap_halves(x):
    w = x.shape[-1]
    lane = lax.broadcasted_iota(jnp.int32, (1, w), 1)
    first_half = (lane % HEAD_DIM) < HALF_DIM
    return jnp.where(first_half, pltpu.roll(x, w - HALF_DIM, 1), pltpu.roll(x, HALF_DIM, 1))


def _rope(x, cos, sin_signed):
    reps = x.shape[-1] // LANES
    if reps > 1:
        cos = jnp.concatenate([cos] * reps, axis=1)
        sin_signed = jnp.concatenate([sin_signed] * reps, axis=1)
    return x * cos + _swap_halves(x) * sin_signed


def _gate_matmuls(xc, wa_ref, wx_ref):
    xcb = xc.astype(BF16)
    half = LRU_WIDTH // 2
    lo, hi = xcb[:, :half], xcb[:, half:]
    ra = jnp.concatenate([_dot(lo, wa_ref[0]), _dot(hi, wa_ref[1])], axis=1)
    ia = jnp.concatenate([_dot(lo, wx_ref[0]), _dot(hi, wx_ref[1])], axis=1)
    return ra, ia


def _gate_nonlin(xc, ra_half, ia_half, b_a, b_x, lam):
    c_half = (-0.5 * LRU_C) * _softplus(-lam)
    t_r = jnp.tanh(ra_half + 0.5 * b_a)
    neg_log_a = (-c_half) * t_r - c_half
    a = jnp.exp2(neg_log_a * (-LOG2E))
    y = jnp.tanh(neg_log_a) * (a * a + 1.0)
    mult = jnp.where(y > 0.0, y * lax.rsqrt(y), 0.0)
    t_i = jnp.tanh(ia_half + 0.5 * b_x)
    hx = 0.5 * xc
    u = mult * (hx * t_i + hx)
    return a, u


def _lane_halves(x, kvh):
    lane = lax.broadcasted_iota(jnp.int32, (1, x.shape[1]), 1)
    low = lane < HEAD_DIM
    rolled = pltpu.roll(x, HEAD_DIM, 1)
    if kvh == 0:
        return jnp.where(low, x, 0.0), jnp.where(low, 0.0, rolled)
    return jnp.where(low, rolled, 0.0), jnp.where(low, 0.0, x)


def _stacked_halves(x, kvh):
    return jnp.concatenate(_lane_halves(x, kvh), axis=0).astype(BF16)


def _sink_softmax_pair(scores, mask, sink_lo, sink_hi):
    s_len = mask.shape[1]
    ps, invs = [], []
    for half, sink in ((0, sink_lo), (1, sink_hi)):
        sh = jnp.where(mask, scores[:, half * s_len:(half + 1) * s_len], -jnp.inf)
        m = jnp.maximum(jnp.max(sh, axis=-1, keepdims=True), sink)
        p = jnp.exp2(sh - m)
        denom = jnp.sum(p, axis=-1, keepdims=True) + jnp.exp2(sink - m)
        ps.append(p.astype(BF16))
        invs.append(1.0 / denom)
    return jnp.concatenate(ps, axis=1), invs


def _pv_pair(probs, vv, invs):
    out = _dot(probs, vv)
    lane = lax.broadcasted_iota(jnp.int32, (1, LANES), 1)
    return out * jnp.where(lane < HEAD_DIM, invs[0], invs[1])


def _scan_groups(a, u, h0):
    rows, width = a.shape
    groups = rows // SUBLANES
    a3 = a.reshape(groups, SUBLANES, width)
    u3 = u.reshape(groups, SUBLANES, width)
    row = lax.broadcasted_iota(jnp.int32, (1, SUBLANES, 1), 1)
    d = 1
    while d < SUBLANES:
        valid = row >= d
        u3 = u3 + a3 * jnp.where(valid, pltpu.roll(u3, d, 1), 0.0)
        a3 = a3 * jnp.where(valid, pltpu.roll(a3, d, 1), 1.0)
        d *= 2
    h = h0
    hs = []
    for g in range(groups):
        hg = u3[g] + a3[g] * h
        h = hg[SUBLANES - 1:SUBLANES, :]
        hs.append(hg)
    return jnp.concatenate(hs, axis=0), h


def _prompt_mixer_kernel(x_ref, cosq_ref, sinq_ref, cosk_ref, sin_k_ref,
                         w_in_ref, b_in_ref, conv_w_ref, conv_b_ref,
                         wa_ref, wx_ref, b_a_ref, b_x_ref, lam_ref, sinks_ref, g_lru_ref, g_attn_ref,
                         w_out_ref, b_out_ref, ln_g_ref, ln_b_ref,
                         h1_ref, conv_out_ref, lru_out_ref, k_out_ref, v_out_ref,
                         xr_ext, h_carry, k_prev, v_prev):
    t_idx = pl.program_id(1)
    n_t = pl.num_programs(1)
    tile = x_ref.shape[0]
    hist = SUBLANES

    @pl.when(t_idx == 0)
    def _():
        xr_ext[0:hist, :] = jnp.zeros((hist, LRU_WIDTH), F32)
        h_carry[...] = jnp.zeros_like(h_carry)
        k_prev[...] = jnp.zeros_like(k_prev)
        v_prev[...] = jnp.zeros_like(v_prev)

    x = x_ref[...]
    xb = x.astype(BF16)

    def proj(c0, width):
        return _dot(xb, w_in_ref[:, c0:c0 + width]) + b_in_ref[:, c0:c0 + width]

    xr = proj(_XR0, LRU_WIDTH)
    xr_ext[hist:hist + tile, :] = xr
    q_raw = proj(_Q0, Q_DIM)
    k_raw = proj(_K0, KV_DIM)
    v = proj(_V0, KV_DIM)

    xc = conv_b_ref[...] + xr * conv_w_ref[CONV_WIDTH - 1:CONV_WIDTH, :]
    for j in range(CONV_WIDTH - 1):
        shift = CONV_WIDTH - 1 - j
        xc = xc + xr_ext[hist - shift:hist - shift + tile, :] * conv_w_ref[j:j + 1, :]
    xr_ext[0:hist, :] = xr_ext[tile:tile + hist, :]

    gate = proj(_GATE0, LRU_WIDTH)
    ra_half, ia_half = _gate_matmuls(xc, wa_ref, wx_ref)

    qb = _rope(q_raw, cosq_ref[...], sinq_ref[...]).astype(BF16)
    k = _rope(k_raw, cosk_ref[...], sin_k_ref[...])

    qi = lax.broadcasted_iota(jnp.int32, (BLOCK_Q, 2 * BLOCK_Q), 0)
    kj = lax.broadcasted_iota(jnp.int32, (BLOCK_Q, 2 * BLOCK_Q), 1)
    band = (kj >= qi) & (kj <= qi + WINDOW)
    first_lo = jnp.where(t_idx == 0, BLOCK_Q, 0)
    band_first = band & (kj >= first_lo)

    n_blocks = tile // BLOCK_Q
    units = []
    for n in range(n_blocks):
        rows = slice(n * BLOCK_Q, (n + 1) * BLOCK_Q)
        if n == 0:
            k2 = jnp.concatenate([k_prev[...], k[rows]], axis=0)
            v2 = jnp.concatenate([v_prev[...], v[rows]], axis=0)
            mask = band_first
        else:
            k2 = k[(n - 1) * BLOCK_Q:(n + 1) * BLOCK_Q]
            v2 = v[(n - 1) * BLOCK_Q:(n + 1) * BLOCK_Q]
            mask = band
        for kvh in range(N_KV_HEADS):
            kk = _stacked_halves(k2, kvh)
            vv = _stacked_halves(v2, kvh)
            for cc in range(2):
                c = 2 * kvh + cc
                units.append((_dot_nt(qb[rows, c * LANES:(c + 1) * LANES], kk), vv, mask, c))
    k_prev[...] = k[tile - BLOCK_Q:tile]
    v_prev[...] = v[tile - BLOCK_Q:tile]

    a, u = _gate_nonlin(xc, ra_half, ia_half, b_a_ref[...], b_x_ref[...], lam_ref[...])

    soft = [_sink_softmax_pair(s, mask, sinks_ref[2 * c] * LOG2E, sinks_ref[2 * c + 1] * LOG2E)
            for s, _, mask, c in units]

    h_all, h_last = _scan_groups(a, u, h_carry[...])
    h_carry[...] = h_last

    outs = [_pv_pair(p, vv, invs) for (p, invs), (_, vv, _, _) in zip(soft, units)]

    y_lru = _rms_norm(_gelu_tanh(gate) * h_all, g_lru_ref[...])
    per_block = N_HEADS // 2
    y_attn = jnp.concatenate(
        [jnp.concatenate(outs[n * per_block:(n + 1) * per_block], axis=1) for n in range(n_blocks)], axis=0)
    y_attn = _rms_norm(y_attn, g_attn_ref[...])

    y_mix = (_dot(y_lru.astype(BF16), w_out_ref[0:LRU_WIDTH, :])
             + _dot(y_attn.astype(BF16), w_out_ref[LRU_WIDTH:LRU_WIDTH + Q_DIM, :]) + b_out_ref[...])
    h1_ref[...] = _layer_norm(ALPHA * x + y_mix, ln_g_ref[...], ln_b_ref[...])

    @pl.when(t_idx == n_t - 1)
    def _():
        conv_out_ref[...] = xr_ext[hist - (CONV_WIDTH - 1):hist, :]
        lru_out_ref[...] = h_last
        k_out_ref[...] = k[tile - WINDOW:tile]
        v_out_ref[...] = v[tile - WINDOW:tile]


def _full_spec(shape):
    zeros = (0,) * len(shape)
    return pl.BlockSpec(shape, lambda *_: zeros, pipeline_mode=pl.Buffered(1))


def _mixer_weight_specs():
    return [
        _full_spec((D_MODEL, IN_DIM)), _full_spec((1, IN_DIM)),
        _full_spec((CONV_WIDTH, LRU_WIDTH)), _full_spec((1, LRU_WIDTH)),
        _full_spec((2, LRU_WIDTH // 2, LRU_WIDTH // 2)), _full_spec((2, LRU_WIDTH // 2, LRU_WIDTH // 2)),
        _full_spec((1, LRU_WIDTH)), _full_spec((1, LRU_WIDTH)), _full_spec((1, LRU_WIDTH)),
        pl.BlockSpec(memory_space=pltpu.SMEM),
        _full_spec((1, LRU_WIDTH)), _full_spec((1, Q_DIM)),
        _full_spec((LRU_WIDTH + Q_DIM, D_MODEL)), _full_spec((1, D_MODEL)),
        _full_spec((1, D_MODEL)), _full_spec((1, D_MODEL)),
    ]


def _prompt_mixer(x, tables, weights):
    bsz, seq, _ = x.shape
    tile = PROMPT_TILE
    assert seq % tile == 0 and tile % BLOCK_Q == 0 and tile >= WINDOW
    n_t = seq // tile
    keep = min(WINDOW, seq)
    table_spec = pl.BlockSpec((tile, LANES), lambda b, t: (t, 0))
    in_specs = [pl.BlockSpec((None, tile, D_MODEL), lambda b, t: (b, t, 0))] + [table_spec] * 4
    in_specs += _mixer_weight_specs()
    out_shape = (
        jax.ShapeDtypeStruct((bsz, seq, D_MODEL), F32),
        jax.ShapeDtypeStruct((bsz, CONV_WIDTH - 1, LRU_WIDTH), F32),
        jax.ShapeDtypeStruct((bsz, 1, LRU_WIDTH), F32),
        jax.ShapeDtypeStruct((bsz, keep, KV_DIM), F32),
        jax.ShapeDtypeStruct((bsz, keep, KV_DIM), F32),
    )
    out_specs = (
        pl.BlockSpec((None, tile, D_MODEL), lambda b, t: (b, t, 0)),
        pl.BlockSpec((None, CONV_WIDTH - 1, LRU_WIDTH), lambda b, t: (b, 0, 0)),
        pl.BlockSpec((None, 1, LRU_WIDTH), lambda b, t: (b, 0, 0)),
        pl.BlockSpec((None, keep, KV_DIM), lambda b, t: (b, 0, 0)),
        pl.BlockSpec((None, keep, KV_DIM), lambda b, t: (b, 0, 0)),
    )
    scratch = [
        pltpu.VMEM((tile + SUBLANES, LRU_WIDTH), F32),
        pltpu.VMEM((1, LRU_WIDTH), F32),
        pltpu.VMEM((BLOCK_Q, KV_DIM), F32),
        pltpu.VMEM((BLOCK_Q, KV_DIM), F32),
    ]
    return pl.pallas_call(
        _prompt_mixer_kernel,
        grid=(bsz, n_t),
        in_specs=in_specs,
        out_specs=out_specs,
        out_shape=out_shape,
        scratch_shapes=scratch,
        compiler_params=pltpu.CompilerParams(
            dimension_semantics=("arbitrary", "arbitrary"), vmem_limit_bytes=VMEM_LIMIT_BYTES),
        name="prompt_mixer",
    )(x, *tables, *weights)


def _ffn_kernel(h_ref, wg_ref, wu_ref, wd_ref, ln_g_ref, ln_b_ref, o_ref):
    slabs = h_ref.shape[1] // D_MODEL
    h = jnp.concatenate([h_ref[:, s * D_MODEL:(s + 1) * D_MODEL] for s in range(slabs)], axis=0)
    hb = h.astype(BF16)
    acc = jnp.zeros(h.shape, F32)
    for c0 in range(0, D_FF, FFN_CHUNK):
        g = _dot(hb, wg_ref[:, c0:c0 + FFN_CHUNK])
        up = _dot(hb, wu_ref[:, c0:c0 + FFN_CHUNK])
        hid = (g * jax.nn.sigmoid(g) * up).astype(BF16)
        acc = acc + _dot(hid, wd_ref[c0:c0 + FFN_CHUNK, :])
    out = _layer_norm(ALPHA * h + acc, ln_g_ref[...], ln_b_ref[...])
    rows = h_ref.shape[0]
    for s in range(slabs):
        o_ref[:, s * D_MODEL:(s + 1) * D_MODEL] = out[s * rows:(s + 1) * rows]


def _ffn(h, wg, wu, wd, ln_g, ln_b, tile):
    rows, width = h.shape
    assert rows % tile == 0 and width % D_MODEL == 0 and D_FF % FFN_CHUNK == 0
    return pl.pallas_call(
        _ffn_kernel,
        grid=(rows // tile,),
        in_specs=[
            pl.BlockSpec((tile, width), lambda i: (i, 0)),
            _full_spec((D_MODEL, D_FF)), _full_spec((D_MODEL, D_FF)), _full_spec((D_FF, D_MODEL)),
            _full_spec((1, D_MODEL)), _full_spec((1, D_MODEL)),
        ],
        out_specs=pl.BlockSpec((tile, width), lambda i: (i, 0)),
        out_shape=jax.ShapeDtypeStruct((rows, width), F32),
        compiler_params=pltpu.CompilerParams(
            dimension_semantics=("arbitrary",), vmem_limit_bytes=VMEM_LIMIT_BYTES),
        name="ffn",
    )(h, wg, wu, wd, ln_g, ln_b)


def _decode_mixer_kernel(x_ref, kc_ref, vc_ref, sconv_ref, slru_ref, cosq_ref, sinq_ref, cosk_ref, sin_k_ref,
                         w_in_ref, b_in_ref, conv_w_ref, conv_b_ref,
                         wa_ref, wx_ref, b_a_ref, b_x_ref, lam_ref, sinks_ref, g_lru_ref, g_attn_ref,
                         w_out_ref, b_out_ref, ln_g_ref, ln_b_ref,
                         h1_ref, conv_out_ref, lru_out_ref, k_out_ref, v_out_ref):
    nb = x_ref.shape[0]
    steps = x_ref.shape[1] // D_MODEL
    win = kc_ref.shape[1]

    x = jnp.concatenate([x_ref[:, t * D_MODEL:(t + 1) * D_MODEL] for t in range(steps)], axis=0)
    xb = x.astype(BF16)

    def proj(c0, width):
        return _dot(xb, w_in_ref[:, c0:c0 + width]) + b_in_ref[:, c0:c0 + width]

    def step_rows(arr, t):
        return arr[t * nb:(t + 1) * nb]

    xr = proj(_XR0, LRU_WIDTH)
    ext = [sconv_ref[:, j * LRU_WIDTH:(j + 1) * LRU_WIDTH] for j in range(CONV_WIDTH - 1)]
    ext += [step_rows(xr, t) for t in range(steps)]
    xc_steps = []
    for t in range(steps):
        acc = conv_b_ref[...]
        for j in range(CONV_WIDTH):
            acc = acc + ext[t + j] * conv_w_ref[j:j + 1, :]
        xc_steps.append(acc)
    for j in range(CONV_WIDTH - 1):
        conv_out_ref[:, j * LRU_WIDTH:(j + 1) * LRU_WIDTH] = ext[steps + j]
    xc = jnp.concatenate(xc_steps, axis=0)

    ra_half, ia_half = _gate_matmuls(xc, wa_ref, wx_ref)
    a, u = _gate_nonlin(xc, ra_half, ia_half, b_a_ref[...], b_x_ref[...], lam_ref[...])
    h = slru_ref[...]
    hs = []
    for t in range(steps):
        h = step_rows(a, t) * h + step_rows(u, t)
        hs.append(h)
    lru_out_ref[...] = h
    gate = proj(_GATE0, LRU_WIDTH)
    y_lru = _rms_norm(_gelu_tanh(gate) * jnp.concatenate(hs, axis=0), g_lru_ref[...])

    def table(ref):
        return jnp.concatenate([jnp.broadcast_to(ref[t:t + 1, :], (nb, LANES)) for t in range(steps)], axis=0)

    q = _rope(proj(_Q0, Q_DIM), table(cosq_ref), table(sinq_ref))
    k = _rope(proj(_K0, KV_DIM), table(cosk_ref), table(sin_k_ref))
    v = proj(_V0, KV_DIM)

    k_out_ref[:, 0:win - steps, :] = kc_ref[:, steps:win, :]
    v_out_ref[:, 0:win - steps, :] = vc_ref[:, steps:win, :]
    for t in range(steps):
        k_out_ref[:, win - steps + t, :] = step_rows(k, t)
        v_out_ref[:, win - steps + t, :] = step_rows(v, t)

    sub = DEC_SUB
    n_sub = nb // sub
    n_cache = sub * win
    n_keys = n_cache + steps * sub
    qr = lax.broadcasted_iota(jnp.int32, (steps * sub, n_keys), 0)
    kc_col = lax.broadcasted_iota(jnp.int32, (steps * sub, n_keys), 1)
    q_step, q_seq = qr // sub, qr % sub
    cached = kc_col < n_cache
    new_col = kc_col - n_cache
    key_seq = jnp.where(cached, kc_col // win, new_col % sub)
    key_lo = jnp.where(cached, q_step + (win - WINDOW), 0)
    key_hi = jnp.where(cached, win - 1, q_step)
    key_idx = jnp.where(cached, kc_col % win, new_col // sub)
    mask = (key_seq == q_seq) & (key_idx >= key_lo) & (key_idx <= key_hi)

    def sub_rows(arr, j):
        return jnp.concatenate([arr[t * nb + j * sub:t * nb + (j + 1) * sub] for t in range(steps)], axis=0)

    attn_sub = []
    for j in range(n_sub):
        qj = sub_rows(q, j).astype(BF16)
        kall = jnp.concatenate([kc_ref[j * sub:(j + 1) * sub].reshape(n_cache, KV_DIM), sub_rows(k, j)], axis=0)
        vall = jnp.concatenate([vc_ref[j * sub:(j + 1) * sub].reshape(n_cache, KV_DIM), sub_rows(v, j)], axis=0)
        tiles = []
        for kvh in range(N_KV_HEADS):
            kk = _stacked_halves(kall, kvh)
            vv = _stacked_halves(vall, kvh)
            for cc in range(2):
                c = 2 * kvh + cc
                scores = _dot_nt(qj[:, c * LANES:(c + 1) * LANES], kk)
                probs, invs = _sink_softmax_pair(scores, mask, sinks_ref[2 * c] * LOG2E, sinks_ref[2 * c + 1] * LOG2E)
                tiles.append(_pv_pair(probs, vv, invs))
        attn_sub.append(jnp.concatenate(tiles, axis=1))
    y_attn = jnp.concatenate(
        [attn_sub[j][t * sub:(t + 1) * sub] for t in range(steps) for j in range(n_sub)], axis=0)
    y_attn = _rms_norm(y_attn, g_attn_ref[...])

    y_mix = (_dot(y_lru.astype(BF16), w_out_ref[0:LRU_WIDTH, :])
             + _dot(y_attn.astype(BF16), w_out_ref[LRU_WIDTH:LRU_WIDTH + Q_DIM, :]) + b_out_ref[...])
    h1 = _layer_norm(ALPHA * x + y_mix, ln_g_ref[...], ln_b_ref[...])
    for t in range(steps):
        h1_ref[:, t * D_MODEL:(t + 1) * D_MODEL] = step_rows(h1, t)


def _decode_mixer(x2d, kc, vc, sconv, slru, tables, weights):
    nseq, width = x2d.shape
    steps = width // D_MODEL
    win = kc.shape[1]
    nb = DEC_BLOCK
    assert nseq % nb == 0 and nb % DEC_SUB == 0 and DEC_SUB % SUBLANES == 0 and steps <= win
    conv_w = (CONV_WIDTH - 1) * LRU_WIDTH
    in_specs = [
        pl.BlockSpec((nb, width), lambda i: (i, 0)),
        pl.BlockSpec((nb, win, KV_DIM), lambda i: (i, 0, 0)),
        pl.BlockSpec((nb, win, KV_DIM), lambda i: (i, 0, 0)),
        pl.BlockSpec((nb, conv_w), lambda i: (i, 0)),
        pl.BlockSpec((nb, LRU_WIDTH), lambda i: (i, 0)),
    ] + [_full_spec((steps, LANES))] * 4 + _mixer_weight_specs()
    out_shape = (
        jax.ShapeDtypeStruct((nseq, width), F32),
        jax.ShapeDtypeStruct((nseq, conv_w), F32),
        jax.ShapeDtypeStruct((nseq, LRU_WIDTH), F32),
        jax.ShapeDtypeStruct((nseq, win, KV_DIM), F32),
        jax.ShapeDtypeStruct((nseq, win, KV_DIM), F32),
    )
    out_specs = (
        pl.BlockSpec((nb, width), lambda i: (i, 0)),
        pl.BlockSpec((nb, conv_w), lambda i: (i, 0)),
        pl.BlockSpec((nb, LRU_WIDTH), lambda i: (i, 0)),
        pl.BlockSpec((nb, win, KV_DIM), lambda i: (i, 0, 0)),
        pl.BlockSpec((nb, win, KV_DIM), lambda i: (i, 0, 0)),
    )
    return pl.pallas_call(
        _decode_mixer_kernel,
        grid=(nseq // nb,),
        in_specs=in_specs,
        out_specs=out_specs,
        out_shape=out_shape,
        compiler_params=pltpu.CompilerParams(
            dimension_semantics=("arbitrary",), vmem_limit_bytes=VMEM_LIMIT_BYTES),
        name="decode_mixer",
    )(x2d, kc, vc, sconv, slru, *tables, *weights)


def _rope_tables(positions):
    inv = ROPE_THETA ** (-jnp.arange(HALF_DIM, dtype=F32) / HALF_DIM)
    ang = positions.astype(F32)[:, None] * inv[None, :]
    cos = jnp.cos(ang)
    sin = jnp.sin(ang)
    reps = LANES // HEAD_DIM
    cos_t = jnp.tile(jnp.concatenate([cos, cos], axis=1), (1, reps))
    sin_t = jnp.tile(jnp.concatenate([-sin, sin], axis=1), (1, reps))
    q_scale = ATTN_SCALE * LOG2E
    return cos_t * q_scale, sin_t * q_scale, cos_t, sin_t


def _block_diag_halves(w):
    per = LRU_BLOCKS // 2
    d = w.shape[-1]
    w4 = w.reshape(2, per, d, d)
    eye = jnp.eye(per, dtype=w.dtype)
    return jnp.einsum('paij,ab->paibj', w4, eye).reshape(2, per * d, per * d)


def kernel(x_prompt, x_sample, cache_k_win, cache_v_win, state_conv, state_lru, w_in, b_in, conv_w, conv_b,
           w_a, b_a, w_x, b_x, lru_lambda, sinks, g_lru, g_attn, w_out, b_out, ln1_g, ln1_b,
           w_gate, w_up, w_down, ln2_g, ln2_b):
    depth = w_in.shape[0]
    assert depth == 1
    l = 0
    bsz, seq, _ = x_prompt.shape
    nseq, steps, _ = x_sample.shape
    win = cache_k_win.shape[2]

    def row(p):
        return p[l].reshape(1, -1)

    mixer_w = (
        w_in[l].astype(BF16), row(b_in), conv_w[l], row(conv_b),
        (0.5 * _block_diag_halves(w_a[l])).astype(BF16), (0.5 * _block_diag_halves(w_x[l])).astype(BF16),
        row(b_a), row(b_x), row(lru_lambda), sinks[l],
        row(g_lru), row(g_attn), w_out[l].astype(BF16), row(b_out), row(ln1_g), row(ln1_b),
    )
    ffn_w = (w_gate[l].astype(BF16), w_up[l].astype(BF16), w_down[l].astype(BF16), row(ln2_g), row(ln2_b))

    h1_p, conv_p, lru_p, k_p, v_p = _prompt_mixer(
        x_prompt, _rope_tables(jnp.arange(seq, dtype=jnp.int32)), mixer_w)
    y_p = _ffn(h1_p.reshape(bsz * seq, D_MODEL), *ffn_w, tile=FFN_TILE).reshape(bsz, seq, D_MODEL)

    h1_s, conv_s, lru_s, k_s, v_s = _decode_mixer(
        x_sample.reshape(nseq, steps * D_MODEL),
        cache_k_win[l].reshape(nseq, win, KV_DIM), cache_v_win[l].reshape(nseq, win, KV_DIM),
        state_conv[l].reshape(nseq, (CONV_WIDTH - 1) * LRU_WIDTH), state_lru[l],
        _rope_tables(PAST_LEN + jnp.arange(steps, dtype=jnp.int32)), mixer_w)
    y_s = _ffn(h1_s, *ffn_w, tile=nseq).reshape(nseq, steps, D_MODEL)

    keep = k_p.shape[1]
    return (
        y_p, y_s,
        conv_p[None], lru_p.reshape(1, bsz, LRU_WIDTH),
        k_p.reshape(1, bsz, keep, N_KV_HEADS, HEAD_DIM), v_p.reshape(1, bsz, keep, N_KV_HEADS, HEAD_DIM),
        conv_s.reshape(1, nseq, CONV_WIDTH - 1, LRU_WIDTH), lru_s[None],
        k_s.reshape(1, nseq, win, N_KV_HEADS, HEAD_DIM), v_s.reshape(1, nseq, win, N_KV_HEADS, HEAD_DIM),
    )
```

```python
import functools

import jax
import jax.numpy as jnp
from jax import lax
from jax.experimental import pallas as pl
from jax.experimental.pallas import tpu as pltpu

D_MODEL = 1024
LRU_WIDTH = 512
LRU_BLOCKS = 8
CONV_WIDTH = 4
LRU_C = 8.0
N_HEADS = 8
N_KV_HEADS = 2
HEAD_DIM = 64
HALF_DIM = HEAD_DIM // 2
Q_DIM = N_HEADS * HEAD_DIM
KV_DIM = N_KV_HEADS * HEAD_DIM
IN_DIM = 2 * LRU_WIDTH + Q_DIM + 2 * KV_DIM
WINDOW = 128
BLOCK_Q = 128
ROPE_THETA = 10000.0
ATTN_SCALE = HEAD_DIM ** -0.5
D_FF = 2816
ALPHA = 2.0 ** 0.25
PAST_LEN = 16384
LOG2E = 1.4426950408889634

LANES = 128
SUBLANES = 8
VMEM_LIMIT_BYTES = 56 * 1024 * 1024

_XR0, _GATE0, _Q0, _K0, _V0 = 0, LRU_WIDTH, 2 * LRU_WIDTH, 2 * LRU_WIDTH + Q_DIM, 2 * LRU_WIDTH + Q_DIM + KV_DIM

PROMPT_TILE = 512
FFN_CHUNK = 256
FFN_HEAD_CHUNKS = 3
DEC_BLOCK = 32
DEC_SUB = 8

BF16 = jnp.bfloat16
F32 = jnp.float32


def _dot(a, b):
    return jnp.dot(a, b, preferred_element_type=F32)


def _dot_nt(a, b):
    return lax.dot_general(a, b, (((1,), (1,)), ((), ())), preferred_element_type=F32)


def _layer_norm(x, g, b, eps=1e-5):
    mu = jnp.mean(x, axis=-1, keepdims=True)
    xc = x - mu
    var = jnp.mean(xc * xc, axis=-1, keepdims=True)
    return xc * lax.rsqrt(var + eps) * g + b


def _rms_norm(x, g, eps=1e-6):
    return x * lax.rsqrt(jnp.mean(x * x, axis=-1, keepdims=True) + eps) * g


def _gelu_tanh(x):
    c = 0.7978845608028654
    return 0.5 * x * (1.0 + jnp.tanh(c * (x + 0.044715 * (x * x * x))))


def _softplus(x):
    return jnp.maximum(x, 0.0) + jnp.log1p(jnp.exp(-jnp.abs(x)))


def _swap_halves(x):
    w = x.shape[-1]
    lane = lax.broadcasted_iota(jnp.int32, (1, w), 1)
    first_half = (lane % HEAD_DIM) < HALF_DIM
    return jnp.where(first_half, pltpu.roll(x, w - HALF_DIM, 1), pltpu.roll(x, HALF_DIM, 1))


def _rope(x, cos, sin_signed):
    reps = x.shape[-1] // LANES
    if reps > 1:
        cos = jnp.concatenate([cos] * reps, axis=1)
        sin_signed = jnp.concatenate([sin_signed] * reps, axis=1)
    return x * cos + _swap_halves(x) * sin_signed


def _gate_matmuls(xc, wa_ref, wx_ref):
    xcb = xc.astype(BF16)
    half = LRU_WIDTH // 2
    lo, hi = xcb[:, :half], xcb[:, half:]
    ra = jnp.concatenate([_dot(lo, wa_ref[0]), _dot(hi, wa_ref[1])], axis=1)
    ia = jnp.concatenate([_dot(lo, wx_ref[0]), _dot(hi, wx_ref[1])], axis=1)
    return ra, ia


def _gate_nonlin(xc, ra_half, ia_half, b_a, b_x, lam):
    c_half = (-0.5 * LRU_C) * _softplus(-lam)
    t_r = jnp.tanh(ra_half + 0.5 * b_a)
    neg_log_a = (-c_half) * t_r - c_half
    a = jnp.exp2(neg_log_a * (-LOG2E))
    y = jnp.tanh(neg_log_a) * (a * a + 1.0)
    mult = jnp.where(y > 0.0, y * lax.rsqrt(y), 0.0)
    t_i = jnp.tanh(ia_half + 0.5 * b_x)
    hx = 0.5 * xc
    u = mult * (hx * t_i + hx)
    return a, u


def _lane_halves(x, kvh):
    lane = lax.broadcasted_iota(jnp.int32, (1, x.shape[1]), 1)
    low = lane < HEAD_DIM
    rolled = pltpu.roll(x, HEAD_DIM, 1)
    if kvh == 0:
        return jnp.where(low, x, 0.0), jnp.where(low, 0.0, rolled)
    return jnp.where(low, rolled, 0.0), jnp.where(low, 0.0, x)


def _stacked_halves(x, kvh):
    return jnp.concatenate(_lane_halves(x, kvh), axis=0).astype(BF16)


def _sink_softmax_pair(scores, mask, sink_lo, sink_hi):
    s_len = mask.shape[1]
    ps, invs = [], []
    for half, sink in ((0, sink_lo), (1, sink_hi)):
        sh = jnp.where(mask, scores[:, half * s_len:(half + 1) * s_len], -jnp.inf)
        m = jnp.maximum(jnp.max(sh, axis=-1, keepdims=True), sink)
        p = jnp.exp2(sh - m)
        denom = jnp.sum(p, axis=-1, keepdims=True) + jnp.exp2(sink - m)
        ps.append(p.astype(BF16))
        invs.append(1.0 / denom)
    return jnp.concatenate(ps, axis=1), invs


def _pv_pair(probs, vv, invs):
    out = _dot(probs, vv)
    lane = lax.broadcasted_iota(jnp.int32, (1, LANES), 1)
    return out * jnp.where(lane < HEAD_DIM, invs[0], invs[1])


def _scan_groups(a, u, h0):
    rows, width = a.shape
    groups = rows // SUBLANES
    a3 = a.reshape(groups, SUBLANES, width)
    u3 = u.reshape(groups, SUBLANES, width)
    row = lax.broadcasted_iota(jnp.int32, (1, SUBLANES, 1), 1)
    d = 1
    while d < SUBLANES:
        valid = row >= d
        u3 = u3 + a3 * jnp.where(valid, pltpu.roll(u3, d, 1), 0.0)
        a3 = a3 * jnp.where(valid, pltpu.roll(a3, d, 1), 1.0)
        d *= 2
    h = h0
    hs = []
    for g in range(groups):
        hg = u3[g] + a3[g] * h
        h = hg[SUBLANES - 1:SUBLANES, :]
        hs.append(hg)
    return jnp.concatenate(hs, axis=0), h


class _FfnStream:
    def __init__(self, lhs, acc, wg_ref, wu_ref, wd_ref, first, last):
        self.lhs, self.acc = lhs, acc
        self.wg_ref, self.wu_ref, self.wd_ref = wg_ref, wu_ref, wd_ref
        self.next, self.last = first, last
        self.pending = []

    def _drain(self):
        gt, up, c = self.pending.pop(0)
        hid = (gt * jax.nn.sigmoid(gt) * up).astype(BF16)
        part = _dot(hid, self.wd_ref[c * FFN_CHUNK:(c + 1) * FFN_CHUNK, :])
        self.acc = part if self.acc is None else self.acc + part

    def emit(self, count):
        for _ in range(count):
            if self.next < self.last:
                cols = slice(self.next * FFN_CHUNK, (self.next + 1) * FFN_CHUNK)
                self.pending.append((_dot(self.lhs, self.wg_ref[:, cols]), _dot(self.lhs, self.wu_ref[:, cols]),
                                     self.next))
                self.next += 1
                if len(self.pending) > 1:
                    self._drain()

    def finish(self):
        self.emit(self.last - self.next)
        while self.pending:
            self._drain()
        return self.acc


def _prompt_layer_kernel(x_ref, cosq_ref, sinq_ref, cosk_ref, sin_k_ref,
                         w_in_ref, b_in_ref, conv_w_ref, conv_b_ref,
                         wa_ref, wx_ref, b_a_ref, b_x_ref, lam_ref, sinks_ref, g_lru_ref, g_attn_ref,
                         w_out_ref, b_out_ref, ln1_g_ref, ln1_b_ref,
                         wg_ref, wu_ref, wd_ref, ln2_g_ref, ln2_b_ref,
                         y_ref, conv_out_ref, lru_out_ref, k_out_ref, v_out_ref,
                         xr_ext, h_carry, k_prev, v_prev, h1_scr, hb_scr, acc_scr, *, n_t, n_tiles):
    g_idx = pl.program_id(0)
    t_idx = lax.rem(g_idx, n_t)
    tile = x_ref.shape[0]
    hist = SUBLANES

    @pl.when(g_idx == 0)
    def _():
        h1_scr[...] = jnp.zeros_like(h1_scr)
        hb_scr[...] = jnp.zeros_like(hb_scr)
        acc_scr[...] = jnp.zeros_like(acc_scr)

    @pl.when(t_idx == 0)
    def _():
        xr_ext[0:hist, :] = jnp.zeros((hist, LRU_WIDTH), F32)
        h_carry[...] = jnp.zeros_like(h_carry)
        k_prev[...] = jnp.zeros_like(k_prev)
        v_prev[...] = jnp.zeros_like(v_prev)

    n_chunks = D_FF // FFN_CHUNK
    prev_ffn = _FfnStream(hb_scr[...], acc_scr[...], wg_ref, wu_ref, wd_ref, FFN_HEAD_CHUNKS, n_chunks)
    ffn_chunks = prev_ffn.emit

    x = x_ref[...]
    xb = x.astype(BF16)

    def proj(c0, width):
        return _dot(xb, w_in_ref[:, c0:c0 + width]) + b_in_ref[:, c0:c0 + width]

    xr = proj(_XR0, LRU_WIDTH)
    xr_ext[hist:hist + tile, :] = xr
    q_raw = proj(_Q0, Q_DIM)
    k_raw = proj(_K0, KV_DIM)
    v = proj(_V0, KV_DIM)
    gate = proj(_GATE0, LRU_WIDTH)

    ffn_chunks(1)
    xc = conv_b_ref[...] + xr * conv_w_ref[CONV_WIDTH - 1:CONV_WIDTH, :]
    for j in range(CONV_WIDTH - 1):
        shift = CONV_WIDTH - 1 - j
        xc = xc + xr_ext[hist - shift:hist - shift + tile, :] * conv_w_ref[j:j + 1, :]
    xr_ext[0:hist, :] = xr_ext[tile:tile + hist, :]

    ra_half, ia_half = _gate_matmuls(xc, wa_ref, wx_ref)

    ffn_chunks(1)
    qb = _rope(q_raw, cosq_ref[...], sinq_ref[...]).astype(BF16)
    k = _rope(k_raw, cosk_ref[...], sin_k_ref[...])

    qi = lax.broadcasted_iota(jnp.int32, (BLOCK_Q, 2 * BLOCK_Q), 0)
    kj = lax.broadcasted_iota(jnp.int32, (BLOCK_Q, 2 * BLOCK_Q), 1)
    band = (kj >= qi) & (kj <= qi + WINDOW)
    first_lo = jnp.where(t_idx == 0, BLOCK_Q, 0)
    band_first = band & (kj >= first_lo)

    n_blocks = tile // BLOCK_Q
    units = []
    for n in range(n_blocks):
        rows = slice(n * BLOCK_Q, (n + 1) * BLOCK_Q)
        if n == 0:
            k2 = jnp.concatenate([k_prev[...], k[rows]], axis=0)
            v2 = jnp.concatenate([v_prev[...], v[rows]], axis=0)
            mask = band_first
        else:
            k2 = k[(n - 1) * BLOCK_Q:(n + 1) * BLOCK_Q]
            v2 = v[(n - 1) * BLOCK_Q:(n + 1) * BLOCK_Q]
            mask = band
        for kvh in range(N_KV_HEADS):
            kk = _stacked_halves(k2, kvh)
            vv = _stacked_halves(v2, kvh)
            for cc in range(2):
                c = 2 * kvh + cc
                units.append((_dot_nt(qb[rows, c * LANES:(c + 1) * LANES], kk), vv, mask, c))
    k_prev[...] = k[tile - BLOCK_Q:tile]
    v_prev[...] = v[tile - BLOCK_Q:tile]

    ffn_chunks(1)
    a, u = _gate_nonlin(xc, ra_half, ia_half, b_a_ref[...], b_x_ref[...], lam_ref[...])

    ffn_chunks(1)
    soft = [_sink_softmax_pair(s, mask, sinks_ref[2 * c] * LOG2E, sinks_ref[2 * c + 1] * LOG2E)
            for s, _, mask, c in units]

    outs = [_pv_pair(p, vv, invs) for (p, invs), (_, vv, _, _) in zip(soft, units)]

    ffn_chunks(2)
    h_all, h_last = _scan_groups(a, u, h_carry[...])
    h_carry[...] = h_last

    y_lru = _rms_norm(_gelu_tanh(gate) * h_all, g_lru_ref[...])
    per_block = N_HEADS // 2
    y_attn = jnp.concatenate(
        [jnp.concatenate(outs[n * per_block:(n + 1) * per_block], axis=1) for n in range(n_blocks)], axis=0)
    y_attn = _rms_norm(y_attn, g_attn_ref[...])

    y_mix = (_dot(y_lru.astype(BF16), w_out_ref[0:LRU_WIDTH, :])
             + _dot(y_attn.astype(BF16), w_out_ref[LRU_WIDTH:LRU_WIDTH + Q_DIM, :]) + b_out_ref[...])
    h1 = _layer_norm(ALPHA * x + y_mix, ln1_g_ref[...], ln1_b_ref[...])
    acc_prev = prev_ffn.finish()
    h_prev = h1_scr[...]
    hb = h1.astype(BF16)
    h1_scr[...] = h1
    hb_scr[...] = hb

    head_ffn = _FfnStream(hb, None, wg_ref, wu_ref, wd_ref, 0, FFN_HEAD_CHUNKS)
    head_ffn.emit(1)
    y_ref[...] = _layer_norm(ALPHA * h_prev + acc_prev, ln2_g_ref[...], ln2_b_ref[...])
    acc_scr[...] = head_ffn.finish()

    @pl.when((t_idx == n_t - 1) & (g_idx < n_tiles))
    def _():
        conv_out_ref[...] = xr_ext[hist - (CONV_WIDTH - 1):hist, :]
        lru_out_ref[...] = h_last
        k_out_ref[...] = k[tile - WINDOW:tile]
        v_out_ref[...] = v[tile - WINDOW:tile]


def _full_spec(shape):
    zeros = (0,) * len(shape)
    return pl.BlockSpec(shape, lambda *_: zeros, pipeline_mode=pl.Buffered(1))


def _mixer_weight_specs():
    return [
        _full_spec((D_MODEL, IN_DIM)), _full_spec((1, IN_DIM)),
        _full_spec((CONV_WIDTH, LRU_WIDTH)), _full_spec((1, LRU_WIDTH)),
        _full_spec((2, LRU_WIDTH // 2, LRU_WIDTH // 2)), _full_spec((2, LRU_WIDTH // 2, LRU_WIDTH // 2)),
        _full_spec((1, LRU_WIDTH)), _full_spec((1, LRU_WIDTH)), _full_spec((1, LRU_WIDTH)),
        pl.BlockSpec(memory_space=pltpu.SMEM),
        _full_spec((1, LRU_WIDTH)), _full_spec((1, Q_DIM)),
        _full_spec((LRU_WIDTH + Q_DIM, D_MODEL)), _full_spec((1, D_MODEL)),
        _full_spec((1, D_MODEL)), _full_spec((1, D_MODEL)),
    ]


def _ffn_weight_specs():
    return [
        _full_spec((D_MODEL, D_FF)), _full_spec((D_MODEL, D_FF)), _full_spec((D_FF, D_MODEL)),
        _full_spec((1, D_MODEL)), _full_spec((1, D_MODEL)),
    ]


def _prompt_layer(x, tables, mixer_w, ffn_w):
    bsz, seq, _ = x.shape
    tile = PROMPT_TILE
    assert seq % tile == 0 and tile % BLOCK_Q == 0 and tile >= WINDOW and D_FF % FFN_CHUNK == 0
    n_t = seq // tile
    n_tiles = bsz * n_t
    keep = min(WINDOW, seq)

    def cur(g):
        gi = jnp.minimum(g, n_tiles - 1)
        return lax.div(gi, n_t), lax.rem(gi, n_t)

    def prev(g):
        go = jnp.maximum(g - 1, 0)
        return lax.div(go, n_t), lax.rem(go, n_t)

    table_spec = pl.BlockSpec((tile, LANES), lambda g: (cur(g)[1], 0))
    in_specs = [pl.BlockSpec((None, tile, D_MODEL), lambda g: (*cur(g), 0))] + [table_spec] * 4
    in_specs += _mixer_weight_specs() + _ffn_weight_specs()
    out_shape = (
        jax.ShapeDtypeStruct((bsz, seq, D_MODEL), F32),
        jax.ShapeDtypeStruct((bsz, CONV_WIDTH - 1, LRU_WIDTH), F32),
        jax.ShapeDtypeStruct((bsz, 1, LRU_WIDTH), F32),
        jax.ShapeDtypeStruct((bsz, keep, KV_DIM), F32),
        jax.ShapeDtypeStruct((bsz, keep, KV_DIM), F32),
    )
    out_specs = (
        pl.BlockSpec((None, tile, D_MODEL), lambda g: (*prev(g), 0)),
        pl.BlockSpec((None, CONV_WIDTH - 1, LRU_WIDTH), lambda g: (cur(g)[0], 0, 0)),
        pl.BlockSpec((None, 1, LRU_WIDTH), lambda g: (cur(g)[0], 0, 0)),
        pl.BlockSpec((None, keep, KV_DIM), lambda g: (cur(g)[0], 0, 0)),
        pl.BlockSpec((None, keep, KV_DIM), lambda g: (cur(g)[0], 0, 0)),
    )
    scratch = [
        pltpu.VMEM((tile + SUBLANES, LRU_WIDTH), F32),
        pltpu.VMEM((1, LRU_WIDTH), F32),
        pltpu.VMEM((BLOCK_Q, KV_DIM), F32),
        pltpu.VMEM((BLOCK_Q, KV_DIM), F32),
        pltpu.VMEM((tile, D_MODEL), F32),
        pltpu.VMEM((tile, D_MODEL), BF16),
        pltpu.VMEM((tile, D_MODEL), F32),
    ]
    return pl.pallas_call(
        functools.partial(_prompt_layer_kernel, n_t=n_t, n_tiles=n_tiles),
        grid=(n_tiles + 1,),
        in_specs=in_specs,
        out_specs=out_specs,
        out_shape=out_shape,
        scratch_shapes=scratch,
        compiler_params=pltpu.CompilerParams(
            dimension_semantics=("arbitrary",), vmem_limit_bytes=VMEM_LIMIT_BYTES),
        name="prompt_layer",
    )(x, *tables, *mixer_w, *ffn_w)


def _ffn_kernel(h_ref, wg_ref, wu_ref, wd_ref, ln_g_ref, ln_b_ref, o_ref):
    slabs = h_ref.shape[1] // D_MODEL
    h = jnp.concatenate([h_ref[:, s * D_MODEL:(s + 1) * D_MODEL] for s in range(slabs)], axis=0)
    hb = h.astype(BF16)
    acc = jnp.zeros(h.shape, F32)
    for c0 in range(0, D_FF, FFN_CHUNK):
        g = _dot(hb, wg_ref[:, c0:c0 + FFN_CHUNK])
        up = _dot(hb, wu_ref[:, c0:c0 + FFN_CHUNK])
        hid = (g * jax.nn.sigmoid(g) * up).astype(BF16)
        acc = acc + _dot(hid, wd_ref[c0:c0 + FFN_CHUNK, :])
    out = _layer_norm(ALPHA * h + acc, ln_g_ref[...], ln_b_ref[...])
    rows = h_ref.shape[0]
    for s in range(slabs):
        o_ref[:, s * D_MODEL:(s + 1) * D_MODEL] = out[s * rows:(s + 1) * rows]


def _ffn(h, wg, wu, wd, ln_g, ln_b, tile):
    rows, width = h.shape
    assert rows % tile == 0 and width % D_MODEL == 0 and D_FF % FFN_CHUNK == 0
    return pl.pallas_call(
        _ffn_kernel,
        grid=(rows // tile,),
        in_specs=[pl.BlockSpec((tile, width), lambda i: (i, 0))] + _ffn_weight_specs(),
        out_specs=pl.BlockSpec((tile, width), lambda i: (i, 0)),
        out_shape=jax.ShapeDtypeStruct((rows, width), F32),
        compiler_params=pltpu.CompilerParams(
            dimension_semantics=("arbitrary",), vmem_limit_bytes=VMEM_LIMIT_BYTES),
        name="ffn",
    )(h, wg, wu, wd, ln_g, ln_b)


def _decode_mixer_kernel(x_ref, kc_ref, vc_ref, sconv_ref, slru_ref, cosq_ref, sinq_ref, cosk_ref, sin_k_ref,
                         w_in_ref, b_in_ref, conv_w_ref, conv_b_ref,
                         wa_ref, wx_ref, b_a_ref, b_x_ref, lam_ref, sinks_ref, g_lru_ref, g_attn_ref,
                         w_out_ref, b_out_ref, ln_g_ref, ln_b_ref,
                         h1_ref, conv_out_ref, lru_out_ref, k_out_ref, v_out_ref):
    nb = x_ref.shape[0]
    steps = x_ref.shape[1] // D_MODEL
    win = kc_ref.shape[1]

    x = jnp.concatenate([x_ref[:, t * D_MODEL:(t + 1) * D_MODEL] for t in range(steps)], axis=0)
    xb = x.astype(BF16)

    def proj(c0, width):
        return _dot(xb, w_in_ref[:, c0:c0 + width]) + b_in_ref[:, c0:c0 + width]

    def step_rows(arr, t):
        return arr[t * nb:(t + 1) * nb]

    xr = proj(_XR0, LRU_WIDTH)
    ext = [sconv_ref[:, j * LRU_WIDTH:(j + 1) * LRU_WIDTH] for j in range(CONV_WIDTH - 1)]
    ext += [step_rows(xr, t) for t in range(steps)]
    xc_steps = []
    for t in range(steps):
        acc = conv_b_ref[...]
        for j in range(CONV_WIDTH):
            acc = acc + ext[t + j] * conv_w_ref[j:j + 1, :]
        xc_steps.append(acc)
    for j in range(CONV_WIDTH - 1):
        conv_out_ref[:, j * LRU_WIDTH:(j + 1) * LRU_WIDTH] = ext[steps + j]
    xc = jnp.concatenate(xc_steps, axis=0)

    ra_half, ia_half = _gate_matmuls(xc, wa_ref, wx_ref)
    a, u = _gate_nonlin(xc, ra_half, ia_half, b_a_ref[...], b_x_ref[...], lam_ref[...])
    h = slru_ref[...]
    hs = []
    for t in range(steps):
        h = step_rows(a, t) * h + step_rows(u, t)
        hs.append(h)
    lru_out_ref[...] = h
    gate = proj(_GATE0, LRU_WIDTH)
    y_lru = _rms_norm(_gelu_tanh(gate) * jnp.concatenate(hs, axis=0), g_lru_ref[...])

    def table(ref):
        return jnp.concatenate([jnp.broadcast_to(ref[t:t + 1, :], (nb, LANES)) for t in range(steps)], axis=0)

    q = _rope(proj(_Q0, Q_DIM), table(cosq_ref), table(sinq_ref))
    k = _rope(proj(_K0, KV_DIM), table(cosk_ref), table(sin_k_ref))
    v = proj(_V0, KV_DIM)

    k_out_ref[:, 0:win - steps, :] = kc_ref[:, steps:win, :]
    v_out_ref[:, 0:win - steps, :] = vc_ref[:, steps:win, :]
    for t in range(steps):
        k_out_ref[:, win - steps + t, :] = step_rows(k, t)
        v_out_ref[:, win - steps + t, :] = step_rows(v, t)

    sub = DEC_SUB
    n_sub = nb // sub
    n_cache = sub * win
    n_keys = n_cache + steps * sub
    qr = lax.broadcasted_iota(jnp.int32, (steps * sub, n_keys), 0)
    kc_col = lax.broadcasted_iota(jnp.int32, (steps * sub, n_keys), 1)
    q_step, q_seq = qr // sub, qr % sub
    cached = kc_col < n_cache
    new_col = kc_col - n_cache
    key_seq = jnp.where(cached, kc_col // win, new_col % sub)
    key_lo = jnp.where(cached, q_step + (win - WINDOW), 0)
    key_hi = jnp.where(cached, win - 1, q_step)
    key_idx = jnp.where(cached, kc_col % win, new_col // sub)
    mask = (key_seq == q_seq) & (key_idx >= key_lo) & (key_idx <= key_hi)

    def sub_rows(arr, j):
        return jnp.concatenate([arr[t * nb + j * sub:t * nb + (j + 1) * sub] for t in range(steps)], axis=0)

    attn_sub = []
    for j in range(n_sub):
        qj = sub_rows(q, j).astype(BF16)
        kall = jnp.concatenate([kc_ref[j * sub:(j + 1) * sub].reshape(n_cache, KV_DIM), sub_rows(k, j)], axis=0)
        vall = jnp.concatenate([vc_ref[j * sub:(j + 1) * sub].reshape(n_cache, KV_DIM), sub_rows(v, j)], axis=0)
        tiles = []
        for kvh in range(N_KV_HEADS):
            kk = _stacked_halves(kall, kvh)
            vv = _stacked_halves(vall, kvh)
            for cc in range(2):
                c = 2 * kvh + cc
                scores = _dot_nt(qj[:, c * LANES:(c + 1) * LANES], kk)
                probs, invs = _sink_softmax_pair(scores, mask, sinks_ref[2 * c] * LOG2E, sinks_ref[2 * c + 1] * LOG2E)
                tiles.append(_pv_pair(probs, vv, invs))
        attn_sub.append(jnp.concatenate(tiles, axis=1))
    y_attn = jnp.concatenate(
        [attn_sub[j][t * sub:(t + 1) * sub] for t in range(steps) for j in range(n_sub)], axis=0)
    y_attn = _rms_norm(y_attn, g_attn_ref[...])

    y_mix = (_dot(y_lru.astype(BF16), w_out_ref[0:LRU_WIDTH, :])
             + _dot(y_attn.astype(BF16), w_out_ref[LRU_WIDTH:LRU_WIDTH + Q_DIM, :]) + b_out_ref[...])
    h1 = _layer_norm(ALPHA * x + y_mix, ln_g_ref[...], ln_b_ref[...])
    for t in range(steps):
        h1_ref[:, t * D_MODEL:(t + 1) * D_MODEL] = step_rows(h1, t)


def _decode_mixer(x2d, kc, vc, sconv, slru, tables, weights):
    nseq, width = x2d.shape
    steps = width // D_MODEL
    win = kc.shape[1]
    nb = DEC_BLOCK
    assert nseq % nb == 0 and nb % DEC_SUB == 0 and DEC_SUB % SUBLANES == 0 and steps <= win
    conv_w = (CONV_WIDTH - 1) * LRU_WIDTH
    in_specs = [
        pl.BlockSpec((nb, width), lambda i: (i, 0)),
        pl.BlockSpec((nb, win, KV_DIM), lambda i: (i, 0, 0)),
        pl.BlockSpec((nb, win, KV_DIM), lambda i: (i, 0, 0)),
        pl.BlockSpec((nb, conv_w), lambda i: (i, 0)),
        pl.BlockSpec((nb, LRU_WIDTH), lambda i: (i, 0)),
    ] + [_full_spec((steps, LANES))] * 4 + _mixer_weight_specs()
    out_shape = (
        jax.ShapeDtypeStruct((nseq, width), F32),
        jax.ShapeDtypeStruct((nseq, conv_w), F32),
        jax.ShapeDtypeStruct((nseq, LRU_WIDTH), F32),
        jax.ShapeDtypeStruct((nseq, win, KV_DIM), F32),
        jax.ShapeDtypeStruct((nseq, win, KV_DIM), F32),
    )
    out_specs = (
        pl.BlockSpec((nb, width), lambda i: (i, 0)),
        pl.BlockSpec((nb, conv_w), lambda i: (i, 0)),
        pl.BlockSpec((nb, LRU_WIDTH), lambda i: (i, 0)),
        pl.BlockSpec((nb, win, KV_DIM), lambda i: (i, 0, 0)),
        pl.BlockSpec((nb, win, KV_DIM), lambda i: (i, 0, 0)),
    )
    return pl.pallas_call(
        _decode_mixer_kernel,
        grid=(nseq // nb,),
        in_specs=in_specs,
        out_specs=out_specs,
        out_shape=out_shape,
        compiler_params=pltpu.CompilerParams(
            dimension_semantics=("arbitrary",), vmem_limit_bytes=VMEM_LIMIT_BYTES),
        name="decode_mixer",
    )(x2d, kc, vc, sconv, slru, *tables, *weights)


def _rope_tables(positions):
    inv = ROPE_THETA ** (-jnp.arange(HALF_DIM, dtype=F32) / HALF_DIM)
    ang = positions.astype(F32)[:, None] * inv[None, :]
    cos = jnp.cos(ang)
    sin = jnp.sin(ang)
    reps = LANES // HEAD_DIM
    cos_t = jnp.tile(jnp.concatenate([cos, cos], axis=1), (1, reps))
    sin_t = jnp.tile(jnp.concatenate([-sin, sin], axis=1), (1, reps))
    q_scale = ATTN_SCALE * LOG2E
    return cos_t * q_scale, sin_t * q_scale, cos_t, sin_t


def _block_diag_halves(w):
    per = LRU_BLOCKS // 2
    d = w.shape[-1]
    w4 = w.reshape(2, per, d, d)
    eye = jnp.eye(per, dtype=w.dtype)
    return jnp.einsum('paij,ab->paibj', w4, eye).reshape(2, per * d, per * d)


def kernel(x_prompt, x_sample, cache_k_win, cache_v_win, state_conv, state_lru, w_in, b_in, conv_w, conv_b,
           w_a, b_a, w_x, b_x, lru_lambda, sinks, g_lru, g_attn, w_out, b_out, ln1_g, ln1_b,
           w_gate, w_up, w_down, ln2_g, ln2_b):
    depth = w_in.shape[0]
    assert depth == 1
    l = 0
    bsz, seq, _ = x_prompt.shape
    nseq, steps, _ = x_sample.shape
    win = cache_k_win.shape[2]

    def row(p):
        return p[l].reshape(1, -1)

    mixer_w = (
        w_in[l].astype(BF16), row(b_in), conv_w[l], row(conv_b),
        (0.5 * _block_diag_halves(w_a[l])).astype(BF16), (0.5 * _block_diag_halves(w_x[l])).astype(BF16),
        row(b_a), row(b_x), row(lru_lambda), sinks[l],
        row(g_lru), row(g_attn), w_out[l].astype(BF16), row(b_out), row(ln1_g), row(ln1_b),
    )
    ffn_w = (w_gate[l].astype(BF16), w_up[l].astype(BF16), w_down[l].astype(BF16), row(ln2_g), row(ln2_b))

    y_p, conv_p, lru_p, k_p, v_p = _prompt_layer(
        x_prompt, _rope_tables(jnp.arange(seq, dtype=jnp.int32)), mixer_w, ffn_w)

    h1_s, conv_s, lru_s, k_s, v_s = _decode_mixer(
        x_sample.reshape(nseq, steps * D_MODEL),
        cache_k_win[l].reshape(nseq, win, KV_DIM), cache_v_win[l].reshape(nseq, win, KV_DIM),
        state_conv[l].reshape(nseq, (CONV_WIDTH - 1) * LRU_WIDTH), state_lru[l],
        _rope_tables(PAST_LEN + jnp.arange(steps, dtype=jnp.int32)), mixer_w)
    y_s = _ffn(h1_s, *ffn_w, tile=nseq).reshape(nseq, steps, D_MODEL)

    keep = k_p.shape[1]
    return (
        y_p, y_s,
        conv_p[None], lru_p.reshape(1, bsz, LRU_WIDTH),
        k_p.reshape(1, bsz, keep, N_KV_HEADS, HEAD_DIM), v_p.reshape(1, bsz, keep, N_KV_HEADS, HEAD_DIM),
        conv_s.reshape(1, nseq, CONV_WIDTH - 1, LRU_WIDTH), lru_s[None],
        k_s.reshape(1, nseq, win, N_KV_HEADS, HEAD_DIM), v_s.reshape(1, nseq, win, N_KV_HEADS, HEAD_DIM),
    )
```

```python
import functools

import jax
import jax.numpy as jnp
from jax import lax
from jax.experimental import pallas as pl
from jax.experimental.pallas import tpu as pltpu

D_MODEL = 1024
LRU_WIDTH = 512
LRU_BLOCKS = 8
CONV_WIDTH = 4
LRU_C = 8.0
N_HEADS = 8
N_KV_HEADS = 2
HEAD_DIM = 64
HALF_DIM = HEAD_DIM // 2
Q_DIM = N_HEADS * HEAD_DIM
KV_DIM = N_KV_HEADS * HEAD_DIM
IN_DIM = 2 * LRU_WIDTH + Q_DIM + 2 * KV_DIM
WINDOW = 128
BLOCK_Q = 128
ROPE_THETA = 10000.0
ATTN_SCALE = HEAD_DIM ** -0.5
D_FF = 2816
ALPHA = 2.0 ** 0.25
PAST_LEN = 16384
LOG2E = 1.4426950408889634
Q_SCALE = ATTN_SCALE * LOG2E

LANES = 128
SUBLANES = 8
VMEM_LIMIT_BYTES = 56 * 1024 * 1024

_XR0, _GATE0, _Q0, _K0, _V0 = 0, LRU_WIDTH, 2 * LRU_WIDTH, 2 * LRU_WIDTH + Q_DIM, 2 * LRU_WIDTH + Q_DIM + KV_DIM

PROMPT_TILE = 512
FFN_CHUNK = 256
FFN_HEAD_CHUNKS = 3
DEC_BLOCK = 32
DEC_SUB = 8

BF16 = jnp.bfloat16
F32 = jnp.float32


def _dot(a, b):
    return jnp.dot(a, b, preferred_element_type=F32)


def _dot_nt(a, b):
    return lax.dot_general(a, b, (((1,), (1,)), ((), ())), preferred_element_type=F32)


def _layer_norm(x, g, b, eps=1e-5):
    mu = jnp.mean(x, axis=-1, keepdims=True)
    xc = x - mu
    var = jnp.mean(xc * xc, axis=-1, keepdims=True)
    return xc * lax.rsqrt(var + eps) * g + b


def _rms_norm(x, g, eps=1e-6):
    return x * lax.rsqrt(jnp.mean(x * x, axis=-1, keepdims=True) + eps) * g


def _gelu_tanh(x):
    c = 0.7978845608028654
    return 0.5 * x * (1.0 + jnp.tanh(c * (x + 0.044715 * (x * x * x))))


def _softplus(x):
    return jnp.maximum(x, 0.0) + jnp.log1p(jnp.exp(-jnp.abs(x)))


def _swap_halves(x):
    w = x.shape[-1]
    lane = lax.broadcasted_iota(jnp.int32, (1, w), 1)
    first_half = (lane % HEAD_DIM) < HALF_DIM
    return jnp.where(first_half, pltpu.roll(x, w - HALF_DIM, 1), pltpu.roll(x, HALF_DIM, 1))


def _rope(x, cos, sin_signed):
    reps = x.shape[-1] // LANES
    if reps > 1:
        cos = jnp.concatenate([cos] * reps, axis=1)
        sin_signed = jnp.concatenate([sin_signed] * reps, axis=1)
    return x * cos + _swap_halves(x) * sin_signed


def _gate_matmuls(xc, wa_ref, wx_ref):
    xcb = xc.astype(BF16)
    half = LRU_WIDTH // 2
    lo, hi = xcb[:, :half], xcb[:, half:]
    ra = jnp.concatenate([_dot(lo, wa_ref[0]), _dot(hi, wa_ref[1])], axis=1)
    ia = jnp.concatenate([_dot(lo, wx_ref[0]), _dot(hi, wx_ref[1])], axis=1)
    return ra, ia


def _gate_nonlin(xc, ra_half, ia_half, b_a, b_x, lam):
    c_half = (-0.5 * LRU_C) * _softplus(-lam)
    t_r = jnp.tanh(ra_half + 0.5 * b_a)
    neg_log_a = (-c_half) * t_r - c_half
    a = jnp.exp2(neg_log_a * (-LOG2E))
    y = jnp.tanh(neg_log_a) * (a * a + 1.0)
    mult = jnp.where(y > 0.0, y * lax.rsqrt(y), 0.0)
    t_i = jnp.tanh(ia_half + 0.5 * b_x)
    hx = 0.5 * xc
    u = mult * (hx * t_i + hx)
    return a, u


def _lane_halves(x, kvh):
    lane = lax.broadcasted_iota(jnp.int32, (1, x.shape[1]), 1)
    low = lane < HEAD_DIM
    rolled = pltpu.roll(x, HEAD_DIM, 1)
    if kvh == 0:
        return jnp.where(low, x, 0.0), jnp.where(low, 0.0, rolled)
    return jnp.where(low, rolled, 0.0), jnp.where(low, 0.0, x)


def _stacked_halves(x, kvh):
    return jnp.concatenate(_lane_halves(x, kvh), axis=0).astype(BF16)


def _sink_softmax_pair(scores, mask, sink_lo, sink_hi):
    s_len = mask.shape[1]
    ps, invs = [], []
    for half, sink in ((0, sink_lo), (1, sink_hi)):
        sh = jnp.where(mask, scores[:, half * s_len:(half + 1) * s_len], -jnp.inf)
        m = jnp.maximum(jnp.max(sh, axis=-1, keepdims=True), sink)
        p = jnp.exp2(sh - m)
        denom = jnp.sum(p, axis=-1, keepdims=True) + jnp.exp2(sink - m)
        ps.append(p.astype(BF16))
        invs.append(1.0 / denom)
    return jnp.concatenate(ps, axis=1), invs


def _pv_pair(probs, vv, invs):
    out = _dot(probs, vv)
    lane = lax.broadcasted_iota(jnp.int32, (1, LANES), 1)
    return out * jnp.where(lane < HEAD_DIM, invs[0], invs[1])


def _scan_groups(a, u, h0):
    rows, width = a.shape
    groups = rows // SUBLANES
    a3 = a.reshape(groups, SUBLANES, width)
    u3 = u.reshape(groups, SUBLANES, width)
    row = lax.broadcasted_iota(jnp.int32, (1, SUBLANES, 1), 1)
    d = 1
    while d < SUBLANES:
        valid = row >= d
        u3 = u3 + a3 * jnp.where(valid, pltpu.roll(u3, d, 1), 0.0)
        a3 = a3 * jnp.where(valid, pltpu.roll(a3, d, 1), 1.0)
        d *= 2
    h = h0
    hs = []
    for g in range(groups):
        hg = u3[g] + a3[g] * h
        h = hg[SUBLANES - 1:SUBLANES, :]
        hs.append(hg)
    return jnp.concatenate(hs, axis=0), h


def _swiglu(gu):
    gt, up = gu[:, :FFN_CHUNK], gu[:, FFN_CHUNK:]
    return (gt * jax.nn.sigmoid(gt) * up).astype(BF16)


class _FfnStream:
    def __init__(self, lhs, acc, wgu_ref, wd_ref, first, last):
        self.lhs, self.acc = lhs, acc
        self.wgu_ref, self.wd_ref = wgu_ref, wd_ref
        self.next, self.last = first, last
        self.pending = []

    def _drain(self):
        gu, c = self.pending.pop(0)
        part = _dot(_swiglu(gu), self.wd_ref[c * FFN_CHUNK:(c + 1) * FFN_CHUNK, :])
        self.acc = part if self.acc is None else self.acc + part

    def emit(self, count):
        for _ in range(count):
            if self.next < self.last:
                cols = slice(self.next * 2 * FFN_CHUNK, (self.next + 1) * 2 * FFN_CHUNK)
                self.pending.append((_dot(self.lhs, self.wgu_ref[:, cols]), self.next))
                self.next += 1
                if len(self.pending) > 1:
                    self._drain()

    def finish(self):
        self.emit(self.last - self.next)
        while self.pending:
            self._drain()
        return self.acc


def _prompt_layer_kernel(x_ref, cos_ref, sin_ref,
                         w_in_ref, b_in_ref, conv_w_ref, conv_b_ref,
                         wa_ref, wx_ref, b_a_ref, b_x_ref, lam_ref, sinks_ref, g_lru_ref, g_attn_ref,
                         w_out_ref, b_out_ref, ln1_g_ref, ln1_b_ref,
                         wgu_ref, wd_ref, ln2_g_ref, ln2_b_ref,
                         y_ref, conv_out_ref, lru_out_ref, k_out_ref, v_out_ref,
                         xr_ext, h_carry, k_prev, v_prev, h1_scr, hb_scr, acc_scr, *, n_t, n_tiles):
    g_idx = pl.program_id(0)
    t_idx = lax.rem(g_idx, n_t)
    tile = x_ref.shape[0]
    hist = SUBLANES

    @pl.when(g_idx == 0)
    def _():
        h1_scr[...] = jnp.zeros_like(h1_scr)
        hb_scr[...] = jnp.zeros_like(hb_scr)
        acc_scr[...] = jnp.zeros_like(acc_scr)

    @pl.when(t_idx == 0)
    def _():
        xr_ext[0:hist, :] = jnp.zeros((hist, LRU_WIDTH), F32)
        h_carry[...] = jnp.zeros_like(h_carry)
        k_prev[...] = jnp.zeros_like(k_prev)
        v_prev[...] = jnp.zeros_like(v_prev)

    n_chunks = D_FF // FFN_CHUNK
    prev_ffn = _FfnStream(hb_scr[...], acc_scr[...], wgu_ref, wd_ref, FFN_HEAD_CHUNKS, n_chunks)
    ffn_chunks = prev_ffn.emit

    x = x_ref[...]
    xb = x.astype(BF16)

    def proj(c0, width):
        return _dot(xb, w_in_ref[:, c0:c0 + width]) + b_in_ref[:, c0:c0 + width]

    xr = proj(_XR0, LRU_WIDTH)
    xr_ext[hist:hist + tile, :] = xr
    q_raw = proj(_Q0, Q_DIM)
    k_raw = proj(_K0, KV_DIM)
    v = proj(_V0, KV_DIM)
    gate = proj(_GATE0, LRU_WIDTH)

    ffn_chunks(1)
    xc = conv_b_ref[...] + xr * conv_w_ref[CONV_WIDTH - 1:CONV_WIDTH, :]
    for j in range(CONV_WIDTH - 1):
        shift = CONV_WIDTH - 1 - j
        xc = xc + xr_ext[hist - shift:hist - shift + tile, :] * conv_w_ref[j:j + 1, :]
    xr_ext[0:hist, :] = xr_ext[tile:tile + hist, :]

    ra_half, ia_half = _gate_matmuls(xc, wa_ref, wx_ref)

    ffn_chunks(1)
    cos, sin = cos_ref[...], sin_ref[...]
    qb = _rope(q_raw, cos * Q_SCALE, sin * Q_SCALE).astype(BF16)
    k = _rope(k_raw, cos, sin)

    qi = lax.broadcasted_iota(jnp.int32, (BLOCK_Q, 2 * BLOCK_Q), 0)
    kj = lax.broadcasted_iota(jnp.int32, (BLOCK_Q, 2 * BLOCK_Q), 1)
    band = (kj >= qi) & (kj <= qi + WINDOW)
    first_lo = jnp.where(t_idx == 0, BLOCK_Q, 0)
    band_first = band & (kj >= first_lo)

    n_blocks = tile // BLOCK_Q
    units = []
    for n in range(n_blocks):
        rows = slice(n * BLOCK_Q, (n + 1) * BLOCK_Q)
        if n == 0:
            k2 = jnp.concatenate([k_prev[...], k[rows]], axis=0)
            v2 = jnp.concatenate([v_prev[...], v[rows]], axis=0)
            mask = band_first
        else:
            k2 = k[(n - 1) * BLOCK_Q:(n + 1) * BLOCK_Q]
            v2 = v[(n - 1) * BLOCK_Q:(n + 1) * BLOCK_Q]
            mask = band
        for kvh in range(N_KV_HEADS):
            kk = _stacked_halves(k2, kvh)
            vv = _stacked_halves(v2, kvh)
            for cc in range(2):
                c = 2 * kvh + cc
                units.append((_dot_nt(qb[rows, c * LANES:(c + 1) * LANES], kk), vv, mask, c))
    k_prev[...] = k[tile - BLOCK_Q:tile]
    v_prev[...] = v[tile - BLOCK_Q:tile]

    ffn_chunks(1)
    a, u = _gate_nonlin(xc, ra_half, ia_half, b_a_ref[...], b_x_ref[...], lam_ref[...])

    ffn_chunks(1)
    soft = [_sink_softmax_pair(s, mask, sinks_ref[2 * c] * LOG2E, sinks_ref[2 * c + 1] * LOG2E)
            for s, _, mask, c in units]

    outs = [_pv_pair(p, vv, invs) for (p, invs), (_, vv, _, _) in zip(soft, units)]

    ffn_chunks(2)
    h_all, h_last = _scan_groups(a, u, h_carry[...])
    h_carry[...] = h_last

    y_lru = _rms_norm(_gelu_tanh(gate) * h_all, g_lru_ref[...])
    per_block = N_HEADS // 2
    y_attn = jnp.concatenate(
        [jnp.concatenate(outs[n * per_block:(n + 1) * per_block], axis=1) for n in range(n_blocks)], axis=0)
    y_attn = _rms_norm(y_attn, g_attn_ref[...])

    y_mix = (_dot(y_lru.astype(BF16), w_out_ref[0:LRU_WIDTH, :])
             + _dot(y_attn.astype(BF16), w_out_ref[LRU_WIDTH:LRU_WIDTH + Q_DIM, :]) + b_out_ref[...])
    h1 = _layer_norm(ALPHA * x + y_mix, ln1_g_ref[...], ln1_b_ref[...])
    acc_prev = prev_ffn.finish()
    h_prev = h1_scr[...]
    hb = h1.astype(BF16)
    h1_scr[...] = h1
    hb_scr[...] = hb

    head_ffn = _FfnStream(hb, None, wgu_ref, wd_ref, 0, FFN_HEAD_CHUNKS)
    head_ffn.emit(1)
    y_ref[...] = _layer_norm(ALPHA * h_prev + acc_prev, ln2_g_ref[...], ln2_b_ref[...])
    acc_scr[...] = head_ffn.finish()

    @pl.when((t_idx == n_t - 1) & (g_idx < n_tiles))
    def _():
        conv_out_ref[...] = xr_ext[hist - (CONV_WIDTH - 1):hist, :]
        lru_out_ref[...] = h_last
        k_out_ref[...] = k[tile - WINDOW:tile]
        v_out_ref[...] = v[tile - WINDOW:tile]


def _full_spec(shape):
    zeros = (0,) * len(shape)
    return pl.BlockSpec(shape, lambda *_: zeros, pipeline_mode=pl.Buffered(1))


def _mixer_weight_specs():
    return [
        _full_spec((D_MODEL, IN_DIM)), _full_spec((1, IN_DIM)),
        _full_spec((CONV_WIDTH, LRU_WIDTH)), _full_spec((1, LRU_WIDTH)),
        _full_spec((2, LRU_WIDTH // 2, LRU_WIDTH // 2)), _full_spec((2, LRU_WIDTH // 2, LRU_WIDTH // 2)),
        _full_spec((1, LRU_WIDTH)), _full_spec((1, LRU_WIDTH)), _full_spec((1, LRU_WIDTH)),
        pl.BlockSpec(memory_space=pltpu.SMEM),
        _full_spec((1, LRU_WIDTH)), _full_spec((1, Q_DIM)),
        _full_spec((LRU_WIDTH + Q_DIM, D_MODEL)), _full_spec((1, D_MODEL)),
        _full_spec((1, D_MODEL)), _full_spec((1, D_MODEL)),
    ]


def _ffn_weight_specs():
    return [
        _full_spec((D_MODEL, 2 * D_FF)), _full_spec((D_FF, D_MODEL)),
        _full_spec((1, D_MODEL)), _full_spec((1, D_MODEL)),
    ]


def _prompt_layer(x, tables, mixer_w, ffn_w):
    bsz, seq, _ = x.shape
    tile = PROMPT_TILE
    assert seq % tile == 0 and tile % BLOCK_Q == 0 and tile >= WINDOW and D_FF % FFN_CHUNK == 0
    n_t = seq // tile
    n_tiles = bsz * n_t
    keep = min(WINDOW, seq)

    def cur(g):
        gi = jnp.minimum(g, n_tiles - 1)
        return lax.div(gi, n_t), lax.rem(gi, n_t)

    def prev(g):
        go = jnp.maximum(g - 1, 0)
        return lax.div(go, n_t), lax.rem(go, n_t)

    table_spec = pl.BlockSpec((tile, LANES), lambda g: (cur(g)[1], 0))
    in_specs = [pl.BlockSpec((None, tile, D_MODEL), lambda g: (*cur(g), 0))] + [table_spec] * 2
    in_specs += _mixer_weight_specs() + _ffn_weight_specs()
    out_shape = (
        jax.ShapeDtypeStruct((bsz, seq, D_MODEL), F32),
        jax.ShapeDtypeStruct((bsz, CONV_WIDTH - 1, LRU_WIDTH), F32),
        jax.ShapeDtypeStruct((bsz, 1, LRU_WIDTH), F32),
        jax.ShapeDtypeStruct((bsz, keep, KV_DIM), F32),
        jax.ShapeDtypeStruct((bsz, keep, KV_DIM), F32),
    )
    out_specs = (
        pl.BlockSpec((None, tile, D_MODEL), lambda g: (*prev(g), 0)),
        pl.BlockSpec((None, CONV_WIDTH - 1, LRU_WIDTH), lambda g: (cur(g)[0], 0, 0)),
        pl.BlockSpec((None, 1, LRU_WIDTH), lambda g: (cur(g)[0], 0, 0)),
        pl.BlockSpec((None, keep, KV_DIM), lambda g: (cur(g)[0], 0, 0)),
        pl.BlockSpec((None, keep, KV_DIM), lambda g: (cur(g)[0], 0, 0)),
    )
    scratch = [
        pltpu.VMEM((tile + SUBLANES, LRU_WIDTH), F32),
        pltpu.VMEM((1, LRU_WIDTH), F32),
        pltpu.VMEM((BLOCK_Q, KV_DIM), F32),
        pltpu.VMEM((BLOCK_Q, KV_DIM), F32),
        pltpu.VMEM((tile, D_MODEL), F32),
        pltpu.VMEM((tile, D_MODEL), BF16),
        pltpu.VMEM((tile, D_MODEL), F32),
    ]
    return pl.pallas_call(
        functools.partial(_prompt_layer_kernel, n_t=n_t, n_tiles=n_tiles),
        grid=(n_tiles + 1,),
        in_specs=in_specs,
        out_specs=out_specs,
        out_shape=out_shape,
        scratch_shapes=scratch,
        compiler_params=pltpu.CompilerParams(
            dimension_semantics=("arbitrary",), vmem_limit_bytes=VMEM_LIMIT_BYTES),
        name="prompt_layer",
    )(x, *tables, *mixer_w, *ffn_w)


def _ffn_kernel(h_ref, wgu_ref, wd_ref, ln_g_ref, ln_b_ref, o_ref):
    slabs = h_ref.shape[1] // D_MODEL
    h = jnp.concatenate([h_ref[:, s * D_MODEL:(s + 1) * D_MODEL] for s in range(slabs)], axis=0)
    acc = _FfnStream(h.astype(BF16), None, wgu_ref, wd_ref, 0, D_FF // FFN_CHUNK).finish()
    out = _layer_norm(ALPHA * h + acc, ln_g_ref[...], ln_b_ref[...])
    rows = h_ref.shape[0]
    for s in range(slabs):
        o_ref[:, s * D_MODEL:(s + 1) * D_MODEL] = out[s * rows:(s + 1) * rows]


def _ffn(h, wgu, wd, ln_g, ln_b, tile):
    rows, width = h.shape
    assert rows % tile == 0 and width % D_MODEL == 0 and D_FF % FFN_CHUNK == 0
    return pl.pallas_call(
        _ffn_kernel,
        grid=(rows // tile,),
        in_specs=[pl.BlockSpec((tile, width), lambda i: (i, 0))] + _ffn_weight_specs(),
        out_specs=pl.BlockSpec((tile, width), lambda i: (i, 0)),
        out_shape=jax.ShapeDtypeStruct((rows, width), F32),
        compiler_params=pltpu.CompilerParams(
            dimension_semantics=("arbitrary",), vmem_limit_bytes=VMEM_LIMIT_BYTES),
        name="ffn",
    )(h, wgu, wd, ln_g, ln_b)


def _decode_mixer_kernel(x_ref, kc_ref, vc_ref, sconv_ref, slru_ref, cos_ref, sin_ref,
                         w_in_ref, b_in_ref, conv_w_ref, conv_b_ref,
                         wa_ref, wx_ref, b_a_ref, b_x_ref, lam_ref, sinks_ref, g_lru_ref, g_attn_ref,
                         w_out_ref, b_out_ref, ln_g_ref, ln_b_ref,
                         h1_ref, conv_out_ref, lru_out_ref, k_out_ref, v_out_ref):
    nb = x_ref.shape[0]
    steps = x_ref.shape[1] // D_MODEL
    win = kc_ref.shape[1]

    x = jnp.concatenate([x_ref[:, t * D_MODEL:(t + 1) * D_MODEL] for t in range(steps)], axis=0)
    xb = x.astype(BF16)

    def proj(c0, width):
        return _dot(xb, w_in_ref[:, c0:c0 + width]) + b_in_ref[:, c0:c0 + width]

    def step_rows(arr, t):
        return arr[t * nb:(t + 1) * nb]

    xr = proj(_XR0, LRU_WIDTH)
    ext = [sconv_ref[:, j * LRU_WIDTH:(j + 1) * LRU_WIDTH] for j in range(CONV_WIDTH - 1)]
    ext += [step_rows(xr, t) for t in range(steps)]
    xc_steps = []
    for t in range(steps):
        acc = conv_b_ref[...]
        for j in range(CONV_WIDTH):
            acc = acc + ext[t + j] * conv_w_ref[j:j + 1, :]
        xc_steps.append(acc)
    for j in range(CONV_WIDTH - 1):
        conv_out_ref[:, j * LRU_WIDTH:(j + 1) * LRU_WIDTH] = ext[steps + j]
    xc = jnp.concatenate(xc_steps, axis=0)

    ra_half, ia_half = _gate_matmuls(xc, wa_ref, wx_ref)
    a, u = _gate_nonlin(xc, ra_half, ia_half, b_a_ref[...], b_x_ref[...], lam_ref[...])
    h = slru_ref[...]
    hs = []
    for t in range(steps):
        h = step_rows(a, t) * h + step_rows(u, t)
        hs.append(h)
    lru_out_ref[...] = h
    gate = proj(_GATE0, LRU_WIDTH)
    y_lru = _rms_norm(_gelu_tanh(gate) * jnp.concatenate(hs, axis=0), g_lru_ref[...])

    def table(ref):
        return jnp.concatenate([jnp.broadcast_to(ref[t:t + 1, :], (nb, LANES)) for t in range(steps)], axis=0)

    cos, sin = table(cos_ref), table(sin_ref)
    q = _rope(proj(_Q0, Q_DIM), cos * Q_SCALE, sin * Q_SCALE)
    k = _rope(proj(_K0, KV_DIM), cos, sin)
    v = proj(_V0, KV_DIM)

    k_out_ref[:, 0:win - steps, :] = kc_ref[:, steps:win, :]
    v_out_ref[:, 0:win - steps, :] = vc_ref[:, steps:win, :]
    for t in range(steps):
        k_out_ref[:, win - steps + t, :] = step_rows(k, t)
        v_out_ref[:, win - steps + t, :] = step_rows(v, t)

    sub = DEC_SUB
    n_sub = nb // sub
    n_cache = sub * win
    n_keys = n_cache + steps * sub
    qr = lax.broadcasted_iota(jnp.int32, (steps * sub, n_keys), 0)
    kc_col = lax.broadcasted_iota(jnp.int32, (steps * sub, n_keys), 1)
    q_step, q_seq = qr // sub, qr % sub
    cached = kc_col < n_cache
    new_col = kc_col - n_cache
    key_seq = jnp.where(cached, kc_col // win, new_col % sub)
    key_lo = jnp.where(cached, q_step + (win - WINDOW), 0)
    key_hi = jnp.where(cached, win - 1, q_step)
    key_idx = jnp.where(cached, kc_col % win, new_col // sub)
    mask = (key_seq == q_seq) & (key_idx >= key_lo) & (key_idx <= key_hi)

    def sub_rows(arr, j):
        return jnp.concatenate([arr[t * nb + j * sub:t * nb + (j + 1) * sub] for t in range(steps)], axis=0)

    units = []
    for j in range(n_sub):
        qj = sub_rows(q, j).astype(BF16)
        kall = jnp.concatenate([kc_ref[j * sub:(j + 1) * sub].reshape(n_cache, KV_DIM), sub_rows(k, j)], axis=0)
        vall = jnp.concatenate([vc_ref[j * sub:(j + 1) * sub].reshape(n_cache, KV_DIM), sub_rows(v, j)], axis=0)
        for kvh in range(N_KV_HEADS):
            kk = _stacked_halves(kall, kvh)
            vv = _stacked_halves(vall, kvh)
            for cc in range(2):
                c = 2 * kvh + cc
                units.append((_dot_nt(qj[:, c * LANES:(c + 1) * LANES], kk), vv, c))
    soft = [_sink_softmax_pair(s, mask, sinks_ref[2 * c] * LOG2E, sinks_ref[2 * c + 1] * LOG2E) for s, _, c in units]
    outs = [_pv_pair(p, vv, invs) for (p, invs), (_, vv, _) in zip(soft, units)]
    per_sub = N_HEADS // 2
    attn_sub = [jnp.concatenate(outs[j * per_sub:(j + 1) * per_sub], axis=1) for j in range(n_sub)]
    y_attn = jnp.concatenate(
        [attn_sub[j][t * sub:(t + 1) * sub] for t in range(steps) for j in range(n_sub)], axis=0)
    y_attn = _rms_norm(y_attn, g_attn_ref[...])

    y_mix = (_dot(y_lru.astype(BF16), w_out_ref[0:LRU_WIDTH, :])
             + _dot(y_attn.astype(BF16), w_out_ref[LRU_WIDTH:LRU_WIDTH + Q_DIM, :]) + b_out_ref[...])
    h1 = _layer_norm(ALPHA * x + y_mix, ln_g_ref[...], ln_b_ref[...])
    for t in range(steps):
        h1_ref[:, t * D_MODEL:(t + 1) * D_MODEL] = step_rows(h1, t)


def _decode_mixer(x2d, kc, vc, sconv, slru, tables, weights):
    nseq, width = x2d.shape
    steps = width // D_MODEL
    win = kc.shape[1]
    nb = DEC_BLOCK
    assert nseq % nb == 0 and nb % DEC_SUB == 0 and DEC_SUB % SUBLANES == 0 and steps <= win
    conv_w = (CONV_WIDTH - 1) * LRU_WIDTH
    in_specs = [
        pl.BlockSpec((nb, width), lambda i: (i, 0)),
        pl.BlockSpec((nb, win, KV_DIM), lambda i: (i, 0, 0)),
        pl.BlockSpec((nb, win, KV_DIM), lambda i: (i, 0, 0)),
        pl.BlockSpec((nb, conv_w), lambda i: (i, 0)),
        pl.BlockSpec((nb, LRU_WIDTH), lambda i: (i, 0)),
    ] + [_full_spec((steps, LANES))] * 2 + _mixer_weight_specs()
    out_shape = (
        jax.ShapeDtypeStruct((nseq, width), F32),
        jax.ShapeDtypeStruct((nseq, conv_w), F32),
        jax.ShapeDtypeStruct((nseq, LRU_WIDTH), F32),
        jax.ShapeDtypeStruct((nseq, win, KV_DIM), F32),
        jax.ShapeDtypeStruct((nseq, win, KV_DIM), F32),
    )
    out_specs = (
        pl.BlockSpec((nb, width), lambda i: (i, 0)),
        pl.BlockSpec((nb, conv_w), lambda i: (i, 0)),
        pl.BlockSpec((nb, LRU_WIDTH), lambda i: (i, 0)),
        pl.BlockSpec((nb, win, KV_DIM), lambda i: (i, 0, 0)),
        pl.BlockSpec((nb, win, KV_DIM), lambda i: (i, 0, 0)),
    )
    return pl.pallas_call(
        _decode_mixer_kernel,
        grid=(nseq // nb,),
        in_specs=in_specs,
        out_specs=out_specs,
        out_shape=out_shape,
        compiler_params=pltpu.CompilerParams(
            dimension_semantics=("arbitrary",), vmem_limit_bytes=VMEM_LIMIT_BYTES),
        name="decode_mixer",
    )(x2d, kc, vc, sconv, slru, *tables, *weights)


def _rope_tables(positions):
    inv = ROPE_THETA ** (-jnp.arange(HALF_DIM, dtype=F32) / HALF_DIM)
    ang = positions.astype(F32)[:, None] * inv[None, :]
    cos = jnp.cos(ang)
    sin = jnp.sin(ang)
    reps = LANES // HEAD_DIM
    cos_t = jnp.tile(jnp.concatenate([cos, cos], axis=1), (1, reps))
    sin_t = jnp.tile(jnp.concatenate([-sin, sin], axis=1), (1, reps))
    return cos_t, sin_t


def _block_diag_halves(w):
    per = LRU_BLOCKS // 2
    d = w.shape[-1]
    w4 = w.reshape(2, per, d, d)
    eye = jnp.eye(per, dtype=w.dtype)
    return jnp.einsum('paij,ab->paibj', w4, eye).reshape(2, per * d, per * d)


def kernel(x_prompt, x_sample, cache_k_win, cache_v_win, state_conv, state_lru, w_in, b_in, conv_w, conv_b,
           w_a, b_a, w_x, b_x, lru_lambda, sinks, g_lru, g_attn, w_out, b_out, ln1_g, ln1_b,
           w_gate, w_up, w_down, ln2_g, ln2_b):
    depth = w_in.shape[0]
    assert depth == 1
    l = 0
    bsz, seq, _ = x_prompt.shape
    nseq, steps, _ = x_sample.shape
    win = cache_k_win.shape[2]

    def row(p):
        return p[l].reshape(1, -1)

    mixer_w = (
        w_in[l].astype(BF16), row(b_in), conv_w[l], row(conv_b),
        (0.5 * _block_diag_halves(w_a[l])).astype(BF16), (0.5 * _block_diag_halves(w_x[l])).astype(BF16),
        row(b_a), row(b_x), row(lru_lambda), sinks[l],
        row(g_lru), row(g_attn), w_out[l].astype(BF16), row(b_out), row(ln1_g), row(ln1_b),
    )
    n_chunks = D_FF // FFN_CHUNK
    w_gu = jnp.stack([w_gate[l].reshape(D_MODEL, n_chunks, FFN_CHUNK),
                      w_up[l].reshape(D_MODEL, n_chunks, FFN_CHUNK)], axis=2).reshape(D_MODEL, 2 * D_FF)
    ffn_w = (w_gu.astype(BF16), w_down[l].astype(BF16), row(ln2_g), row(ln2_b))

    y_p, conv_p, lru_p, k_p, v_p = _prompt_layer(
        x_prompt, _rope_tables(jnp.arange(seq, dtype=jnp.int32)), mixer_w, ffn_w)

    h1_s, conv_s, lru_s, k_s, v_s = _decode_mixer(
        x_sample.reshape(nseq, steps * D_MODEL),
        cache_k_win[l].reshape(nseq, win, KV_DIM), cache_v_win[l].reshape(nseq, win, KV_DIM),
        state_conv[l].reshape(nseq, (CONV_WIDTH - 1) * LRU_WIDTH), state_lru[l],
        _rope_tables(PAST_LEN + jnp.arange(steps, dtype=jnp.int32)), mixer_w)
    y_s = _ffn(h1_s, *ffn_w, tile=nseq).reshape(nseq, steps, D_MODEL)

    keep = k_p.shape[1]
    return (
        y_p, y_s,
        conv_p[None], lru_p.reshape(1, bsz, LRU_WIDTH),
        k_p.reshape(1, bsz, keep, N_KV_HEADS, HEAD_DIM), v_p.reshape(1, bsz, keep, N_KV_HEADS, HEAD_DIM),
        conv_s.reshape(1, nseq, CONV_WIDTH - 1, LRU_WIDTH), lru_s[None],
        k_s.reshape(1, nseq, win, N_KV_HEADS, HEAD_DIM), v_s.reshape(1, nseq, win, N_KV_HEADS, HEAD_DIM),
    )
```

```python
import functools

import jax
import jax.numpy as jnp
from jax import lax
from jax.experimental import pallas as pl
from jax.experimental.pallas import tpu as pltpu

D_MODEL = 1024
LRU_WIDTH = 512
LRU_BLOCKS = 8
CONV_WIDTH = 4
LRU_C = 8.0
N_HEADS = 8
N_KV_HEADS = 2
HEAD_DIM = 64
HALF_DIM = HEAD_DIM // 2
Q_DIM = N_HEADS * HEAD_DIM
KV_DIM = N_KV_HEADS * HEAD_DIM
IN_DIM = 2 * LRU_WIDTH + Q_DIM + 2 * KV_DIM
WINDOW = 128
BLOCK_Q = 128
ROPE_THETA = 10000.0
ATTN_SCALE = HEAD_DIM ** -0.5
D_FF = 2816
ALPHA = 2.0 ** 0.25
PAST_LEN = 16384
LOG2E = 1.4426950408889634
Q_SCALE = ATTN_SCALE * LOG2E

LANES = 128
SUBLANES = 8
VMEM_LIMIT_BYTES = 56 * 1024 * 1024

_XR0, _GATE0, _Q0, _K0, _V0 = 0, LRU_WIDTH, 2 * LRU_WIDTH, 2 * LRU_WIDTH + Q_DIM, 2 * LRU_WIDTH + Q_DIM + KV_DIM

PROMPT_TILE = 512
FFN_CHUNK = 256
FFN_HEAD_CHUNKS = 3
FFN_DOWN_GROUP = 1
DEC_BLOCK = 32
DEC_SUB = 8

BF16 = jnp.bfloat16
F32 = jnp.float32

_VEC_FIELDS = (
    ("b_in", IN_DIM), ("conv_w0", LRU_WIDTH), ("conv_w1", LRU_WIDTH), ("conv_w2", LRU_WIDTH), ("conv_w3", LRU_WIDTH),
    ("conv_b", LRU_WIDTH), ("b_a", LRU_WIDTH), ("b_x", LRU_WIDTH), ("lam", LRU_WIDTH),
    ("g_lru", LRU_WIDTH), ("g_attn", Q_DIM), ("b_out", D_MODEL),
    ("ln1_g", D_MODEL), ("ln1_b", D_MODEL), ("ln2_g", D_MODEL), ("ln2_b", D_MODEL),
)
_VEC_LAYOUT = {}
_VEC_LEN = 0
for _name, _n in _VEC_FIELDS:
    assert _n % 128 == 0
    _VEC_LAYOUT[_name] = (_VEC_LEN, _n)
    _VEC_LEN += _n


def _vec_views(vec_ref):
    return {name: vec_ref.at[:, off:off + n] for name, (off, n) in _VEC_LAYOUT.items()}


def _dot(a, b):
    return jnp.dot(a, b, preferred_element_type=F32)


def _dot_nt(a, b):
    return lax.dot_general(a, b, (((1,), (1,)), ((), ())), preferred_element_type=F32)


def _layer_norm(x, g, b, eps=1e-5):
    mu = jnp.mean(x, axis=-1, keepdims=True)
    xc = x - mu
    var = jnp.mean(xc * xc, axis=-1, keepdims=True)
    return xc * lax.rsqrt(var + eps) * g + b


def _rms_norm(x, g, eps=1e-6):
    return x * lax.rsqrt(jnp.mean(x * x, axis=-1, keepdims=True) + eps) * g


def _gelu_tanh(x):
    c = 0.7978845608028654
    return 0.5 * x * (1.0 + jnp.tanh(c * (x + 0.044715 * (x * x * x))))


def _softplus(x):
    return jnp.maximum(x, 0.0) + jnp.log1p(jnp.exp(-jnp.abs(x)))


def _swap_halves(x):
    w = x.shape[-1]
    lane = lax.broadcasted_iota(jnp.int32, (1, w), 1)
    first_half = (lane % HEAD_DIM) < HALF_DIM
    return jnp.where(first_half, pltpu.roll(x, w - HALF_DIM, 1), pltpu.roll(x, HALF_DIM, 1))


def _rope(x, cos, sin_signed):
    reps = x.shape[-1] // LANES
    if reps > 1:
        cos = jnp.concatenate([cos] * reps, axis=1)
        sin_signed = jnp.concatenate([sin_signed] * reps, axis=1)
    return x * cos + _swap_halves(x) * sin_signed


def _gate_matmuls(xc, wa_ref, wx_ref):
    xcb = xc.astype(BF16)
    half = LRU_WIDTH // 2
    lo, hi = xcb[:, :half], xcb[:, half:]
    ra = jnp.concatenate([_dot(lo, wa_ref[0]), _dot(hi, wa_ref[1])], axis=1)
    ia = jnp.concatenate([_dot(lo, wx_ref[0]), _dot(hi, wx_ref[1])], axis=1)
    return ra, ia


def _gate_nonlin(xc, ra_half, ia_half, b_a, b_x, lam):
    c_half = (-0.5 * LRU_C) * _softplus(-lam)
    t_r = jnp.tanh(ra_half + 0.5 * b_a)
    neg_log_a = (-c_half) * t_r - c_half
    a = jnp.exp2(neg_log_a * (-LOG2E))
    y = jnp.tanh(neg_log_a) * (a * a + 1.0)
    mult = jnp.where(y > 0.0, y * lax.rsqrt(y), 0.0)
    t_i = jnp.tanh(ia_half + 0.5 * b_x)
    hx = 0.5 * xc
    u = mult * (hx * t_i + hx)
    return a, u


def _lane_halves(x, kvh):
    lane = lax.broadcasted_iota(jnp.int32, (1, x.shape[1]), 1)
    low = lane < HEAD_DIM
    rolled = pltpu.roll(x, HEAD_DIM, 1)
    if kvh == 0:
        return jnp.where(low, x, 0.0), jnp.where(low, 0.0, rolled)
    return jnp.where(low, rolled, 0.0), jnp.where(low, 0.0, x)


def _stacked_halves(x, kvh):
    return jnp.concatenate(_lane_halves(x, kvh), axis=0).astype(BF16)


def _sink_softmax_pair(scores, mask, sink_lo, sink_hi):
    s_len = mask.shape[1]
    ps, invs = [], []
    for half, sink in ((0, sink_lo), (1, sink_hi)):
        sh = jnp.where(mask, scores[:, half * s_len:(half + 1) * s_len], -jnp.inf)
        m = jnp.maximum(jnp.max(sh, axis=-1, keepdims=True), sink)
        p = jnp.exp2(sh - m)
        denom = jnp.sum(p, axis=-1, keepdims=True) + jnp.exp2(sink - m)
        ps.append(p.astype(BF16))
        invs.append(1.0 / denom)
    return jnp.concatenate(ps, axis=1), invs


def _pv_pair(probs, vv, invs):
    out = _dot(probs, vv)
    lane = lax.broadcasted_iota(jnp.int32, (1, LANES), 1)
    return out * jnp.where(lane < HEAD_DIM, invs[0], invs[1])


def _scan_groups(a, u, h0):
    rows, width = a.shape
    groups = rows // SUBLANES
    a3 = a.reshape(groups, SUBLANES, width)
    u3 = u.reshape(groups, SUBLANES, width)
    row = lax.broadcasted_iota(jnp.int32, (1, SUBLANES, 1), 1)
    d = 1
    while d < SUBLANES:
        valid = row >= d
        u3 = u3 + a3 * jnp.where(valid, pltpu.roll(u3, d, 1), 0.0)
        a3 = a3 * jnp.where(valid, pltpu.roll(a3, d, 1), 1.0)
        d *= 2
    h = h0
    hs = []
    for g in range(groups):
        hg = u3[g] + a3[g] * h
        h = hg[SUBLANES - 1:SUBLANES, :]
        hs.append(hg)
    return jnp.concatenate(hs, axis=0), h


class _FfnStream:
    def __init__(self, lhs, acc, wg_ref, wu_ref, wd_ref, first, last):
        self.lhs, self.acc = lhs, acc
        self.wg_ref, self.wu_ref, self.wd_ref = wg_ref, wu_ref, wd_ref
        self.next, self.last = first, last
        self.pending = []
        self.hids = []
        self.hid0 = first

    def _activate(self):
        gt, up = self.pending.pop(0)
        self.hids.append((gt * jax.nn.sigmoid(gt) * up).astype(BF16))
        if len(self.hids) == FFN_DOWN_GROUP:
            self._down()

    def _down(self):
        rows = slice(self.hid0 * FFN_CHUNK, (self.hid0 + len(self.hids)) * FFN_CHUNK)
        part = _dot(jnp.concatenate(self.hids, axis=1), self.wd_ref[rows, :])
        self.acc = part if self.acc is None else self.acc + part
        self.hid0 += len(self.hids)
        self.hids = []

    def emit(self, count):
        for _ in range(count):
            if self.next < self.last:
                cols = slice(self.next * FFN_CHUNK, (self.next + 1) * FFN_CHUNK)
                self.pending.append((_dot(self.lhs, self.wg_ref[:, cols]), _dot(self.lhs, self.wu_ref[:, cols])))
                self.next += 1
                if len(self.pending) > 1:
                    self._activate()

    def finish(self):
        self.emit(self.last - self.next)
        while self.pending:
            self._activate()
        if self.hids:
            self._down()
        return self.acc


def _prompt_layer_kernel(x_ref, cos_ref, sin_ref, vec_ref, sinks_ref,
                         w_in_ref, wa_ref, wx_ref, w_out_ref, wg_ref, wu_ref, wd_ref,
                         y_ref, conv_out_ref, lru_out_ref, k_out_ref, v_out_ref,
                         xr_ext, h_carry, k_prev, v_prev, h1_scr, hb_scr, acc_scr, *, n_t, n_tiles):
    vec = _vec_views(vec_ref)
    b_in_ref, conv_b_ref, b_a_ref, b_x_ref, lam_ref = (vec[n] for n in ("b_in", "conv_b", "b_a", "b_x", "lam"))
    g_lru_ref, g_attn_ref, b_out_ref = (vec[n] for n in ("g_lru", "g_attn", "b_out"))
    ln1_g_ref, ln1_b_ref, ln2_g_ref, ln2_b_ref = (vec[n] for n in ("ln1_g", "ln1_b", "ln2_g", "ln2_b"))
    conv_w_rows = [vec["conv_w%d" % j] for j in range(CONV_WIDTH)]
    g_idx = pl.program_id(0)
    t_idx = lax.rem(g_idx, n_t)
    tile = x_ref.shape[0]
    hist = SUBLANES

    @pl.when(g_idx == 0)
    def _():
        h1_scr[...] = jnp.zeros_like(h1_scr)
        hb_scr[...] = jnp.zeros_like(hb_scr)
        acc_scr[...] = jnp.zeros_like(acc_scr)

    @pl.when(t_idx == 0)
    def _():
        xr_ext[0:hist, :] = jnp.zeros((hist, LRU_WIDTH), F32)
        h_carry[...] = jnp.zeros_like(h_carry)
        k_prev[...] = jnp.zeros_like(k_prev)
        v_prev[...] = jnp.zeros_like(v_prev)

    n_chunks = D_FF // FFN_CHUNK
    prev_ffn = _FfnStream(hb_scr[...], acc_scr[...], wg_ref, wu_ref, wd_ref, FFN_HEAD_CHUNKS, n_chunks)
    ffn_chunks = prev_ffn.emit

    x = x_ref[...]
    xb = x.astype(BF16)

    def proj(c0, width):
        return _dot(xb, w_in_ref[:, c0:c0 + width]) + b_in_ref[:, c0:c0 + width]

    xr = proj(_XR0, LRU_WIDTH)
    xr_ext[hist:hist + tile, :] = xr
    q_raw = proj(_Q0, Q_DIM)
    k_raw = proj(_K0, KV_DIM)
    v = proj(_V0, KV_DIM)
    gate = proj(_GATE0, LRU_WIDTH)

    ffn_chunks(1)
    xc = conv_b_ref[...] + xr * conv_w_rows[CONV_WIDTH - 1][...]
    for j in range(CONV_WIDTH - 1):
        shift = CONV_WIDTH - 1 - j
        xc = xc + xr_ext[hist - shift:hist - shift + tile, :] * conv_w_rows[j][...]
    xr_ext[0:hist, :] = xr_ext[tile:tile + hist, :]

    ra_half, ia_half = _gate_matmuls(xc, wa_ref, wx_ref)

    ffn_chunks(1)
    cos, sin = cos_ref[...], sin_ref[...]
    qb = _rope(q_raw, cos * Q_SCALE, sin * Q_SCALE).astype(BF16)
    k = _rope(k_raw, cos, sin)

    qi = lax.broadcasted_iota(jnp.int32, (BLOCK_Q, 2 * BLOCK_Q), 0)
    kj = lax.broadcasted_iota(jnp.int32, (BLOCK_Q, 2 * BLOCK_Q), 1)
    band = (kj >= qi) & (kj <= qi + WINDOW)
    first_lo = jnp.where(t_idx == 0, BLOCK_Q, 0)
    band_first = band & (kj >= first_lo)

    n_blocks = tile // BLOCK_Q
    units = []
    for n in range(n_blocks):
        rows = slice(n * BLOCK_Q, (n + 1) * BLOCK_Q)
        if n == 0:
            k2 = jnp.concatenate([k_prev[...], k[rows]], axis=0)
            v2 = jnp.concatenate([v_prev[...], v[rows]], axis=0)
            mask = band_first
        else:
            k2 = k[(n - 1) * BLOCK_Q:(n + 1) * BLOCK_Q]
            v2 = v[(n - 1) * BLOCK_Q:(n + 1) * BLOCK_Q]
            mask = band
        for kvh in range(N_KV_HEADS):
            kk = _stacked_halves(k2, kvh)
            vv = _stacked_halves(v2, kvh)
            for cc in range(2):
                c = 2 * kvh + cc
                units.append((_dot_nt(qb[rows, c * LANES:(c + 1) * LANES], kk), vv, mask, c))
    k_prev[...] = k[tile - BLOCK_Q:tile]
    v_prev[...] = v[tile - BLOCK_Q:tile]

    ffn_chunks(1)
    a, u = _gate_nonlin(xc, ra_half, ia_half, b_a_ref[...], b_x_ref[...], lam_ref[...])

    ffn_chunks(1)
    soft = [_sink_softmax_pair(s, mask, sinks_ref[2 * c] * LOG2E, sinks_ref[2 * c + 1] * LOG2E)
            for s, _, mask, c in units]

    outs = [_pv_pair(p, vv, invs) for (p, invs), (_, vv, _, _) in zip(soft, units)]

    ffn_chunks(2)
    h_all, h_last = _scan_groups(a, u, h_carry[...])
    h_carry[...] = h_last

    y_lru = _rms_norm(_gelu_tanh(gate) * h_all, g_lru_ref[...])
    per_block = N_HEADS // 2
    y_attn = jnp.concatenate(
        [jnp.concatenate(outs[n * per_block:(n + 1) * per_block], axis=1) for n in range(n_blocks)], axis=0)
    y_attn = _rms_norm(y_attn, g_attn_ref[...])

    y_mix = (_dot(y_lru.astype(BF16), w_out_ref[0:LRU_WIDTH, :])
             + _dot(y_attn.astype(BF16), w_out_ref[LRU_WIDTH:LRU_WIDTH + Q_DIM, :]) + b_out_ref[...])
    h1 = _layer_norm(ALPHA * x + y_mix, ln1_g_ref[...], ln1_b_ref[...])
    acc_prev = prev_ffn.finish()
    h_prev = h1_scr[...]
    hb = h1.astype(BF16)
    h1_scr[...] = h1
    hb_scr[...] = hb

    head_ffn = _FfnStream(hb, None, wg_ref, wu_ref, wd_ref, 0, FFN_HEAD_CHUNKS)
    head_ffn.emit(1)
    y_ref[...] = _layer_norm(ALPHA * h_prev + acc_prev, ln2_g_ref[...], ln2_b_ref[...])
    acc_scr[...] = head_ffn.finish()

    @pl.when((t_idx == n_t - 1) & (g_idx < n_tiles))
    def _():
        conv_out_ref[...] = xr_ext[hist - (CONV_WIDTH - 1):hist, :]
        lru_out_ref[...] = h_last
        k_out_ref[...] = k[tile - WINDOW:tile]
        v_out_ref[...] = v[tile - WINDOW:tile]


def _full_spec(shape):
    zeros = (0,) * len(shape)
    return pl.BlockSpec(shape, lambda *_: zeros, pipeline_mode=pl.Buffered(1))


def _vec_spec():
    return _full_spec((1, _VEC_LEN))


def _mixer_weight_specs():
    return [
        _vec_spec(), pl.BlockSpec(memory_space=pltpu.SMEM),
        _full_spec((D_MODEL, IN_DIM)),
        _full_spec((2, LRU_WIDTH // 2, LRU_WIDTH // 2)), _full_spec((2, LRU_WIDTH // 2, LRU_WIDTH // 2)),
        _full_spec((LRU_WIDTH + Q_DIM, D_MODEL)),
    ]


def _ffn_weight_specs():
    return [_full_spec((D_MODEL, D_FF)), _full_spec((D_MODEL, D_FF)), _full_spec((D_FF, D_MODEL))]


def _prompt_layer(x, tables, mixer_w, ffn_w):
    bsz, seq, _ = x.shape
    tile = PROMPT_TILE
    assert seq % tile == 0 and tile % BLOCK_Q == 0 and tile >= WINDOW and D_FF % FFN_CHUNK == 0
    n_t = seq // tile
    n_tiles = bsz * n_t
    keep = min(WINDOW, seq)

    def cur(g):
        gi = jnp.minimum(g, n_tiles - 1)
        return lax.div(gi, n_t), lax.rem(gi, n_t)

    def prev(g):
        go = jnp.maximum(g - 1, 0)
        return lax.div(go, n_t), lax.rem(go, n_t)

    table_spec = pl.BlockSpec((tile, LANES), lambda g: (cur(g)[1], 0))
    in_specs = [pl.BlockSpec((None, tile, D_MODEL), lambda g: (*cur(g), 0))] + [table_spec] * 2
    in_specs += _mixer_weight_specs() + _ffn_weight_specs()
    out_shape = (
        jax.ShapeDtypeStruct((bsz, seq, D_MODEL), F32),
        jax.ShapeDtypeStruct((bsz, CONV_WIDTH - 1, LRU_WIDTH), F32),
        jax.ShapeDtypeStruct((bsz, 1, LRU_WIDTH), F32),
        jax.ShapeDtypeStruct((bsz, keep, KV_DIM), F32),
        jax.ShapeDtypeStruct((bsz, keep, KV_DIM), F32),
    )
    out_specs = (
        pl.BlockSpec((None, tile, D_MODEL), lambda g: (*prev(g), 0)),
        pl.BlockSpec((None, CONV_WIDTH - 1, LRU_WIDTH), lambda g: (cur(g)[0], 0, 0)),
        pl.BlockSpec((None, 1, LRU_WIDTH), lambda g: (cur(g)[0], 0, 0)),
        pl.BlockSpec((None, keep, KV_DIM), lambda g: (cur(g)[0], 0, 0)),
        pl.BlockSpec((None, keep, KV_DIM), lambda g: (cur(g)[0], 0, 0)),
    )
    scratch = [
        pltpu.VMEM((tile + SUBLANES, LRU_WIDTH), F32),
        pltpu.VMEM((1, LRU_WIDTH), F32),
        pltpu.VMEM((BLOCK_Q, KV_DIM), F32),
        pltpu.VMEM((BLOCK_Q, KV_DIM), F32),
        pltpu.VMEM((tile, D_MODEL), F32),
        pltpu.VMEM((tile, D_MODEL), BF16),
        pltpu.VMEM((tile, D_MODEL), F32),
    ]
    return pl.pallas_call(
        functools.partial(_prompt_layer_kernel, n_t=n_t, n_tiles=n_tiles),
        grid=(n_tiles + 1,),
        in_specs=in_specs,
        out_specs=out_specs,
        out_shape=out_shape,
        scratch_shapes=scratch,
        compiler_params=pltpu.CompilerParams(
            dimension_semantics=("arbitrary",), vmem_limit_bytes=VMEM_LIMIT_BYTES),
        name="prompt_layer",
    )(x, *tables, *mixer_w, *ffn_w)


def _ffn_kernel(h_ref, vec_ref, wg_ref, wu_ref, wd_ref, o_ref):
    vec = _vec_views(vec_ref)
    ln_g_ref, ln_b_ref = vec["ln2_g"], vec["ln2_b"]
    slabs = h_ref.shape[1] // D_MODEL
    h = jnp.concatenate([h_ref[:, s * D_MODEL:(s + 1) * D_MODEL] for s in range(slabs)], axis=0)
    acc = _FfnStream(h.astype(BF16), None, wg_ref, wu_ref, wd_ref, 0, D_FF // FFN_CHUNK).finish()
    out = _layer_norm(ALPHA * h + acc, ln_g_ref[...], ln_b_ref[...])
    rows = h_ref.shape[0]
    for s in range(slabs):
        o_ref[:, s * D_MODEL:(s + 1) * D_MODEL] = out[s * rows:(s + 1) * rows]


def _ffn(h, vec, wg, wu, wd, tile):
    rows, width = h.shape
    assert rows % tile == 0 and width % D_MODEL == 0 and D_FF % FFN_CHUNK == 0
    return pl.pallas_call(
        _ffn_kernel,
        grid=(rows // tile,),
        in_specs=[pl.BlockSpec((tile, width), lambda i: (i, 0)), _vec_spec()] + _ffn_weight_specs(),
        out_specs=pl.BlockSpec((tile, width), lambda i: (i, 0)),
        out_shape=jax.ShapeDtypeStruct((rows, width), F32),
        compiler_params=pltpu.CompilerParams(
            dimension_semantics=("arbitrary",), vmem_limit_bytes=VMEM_LIMIT_BYTES),
        name="ffn",
    )(h, vec, wg, wu, wd)


def _decode_mixer_kernel(x_ref, kc_ref, vc_ref, sconv_ref, slru_ref, cos_ref, sin_ref, vec_ref, sinks_ref,
                         w_in_ref, wa_ref, wx_ref, w_out_ref,
                         h1_ref, conv_out_ref, lru_out_ref, k_out_ref, v_out_ref):
    vec = _vec_views(vec_ref)
    b_in_ref, conv_b_ref, b_a_ref, b_x_ref, lam_ref = (vec[n] for n in ("b_in", "conv_b", "b_a", "b_x", "lam"))
    g_lru_ref, g_attn_ref, b_out_ref, ln_g_ref, ln_b_ref = (
        vec[n] for n in ("g_lru", "g_attn", "b_out", "ln1_g", "ln1_b"))
    conv_w_rows = [vec["conv_w%d" % j] for j in range(CONV_WIDTH)]
    nb = x_ref.shape[0]
    steps = x_ref.shape[1] // D_MODEL
    win = kc_ref.shape[1]

    x = jnp.concatenate([x_ref[:, t * D_MODEL:(t + 1) * D_MODEL] for t in range(steps)], axis=0)
    xb = x.astype(BF16)

    def proj(c0, width):
        return _dot(xb, w_in_ref[:, c0:c0 + width]) + b_in_ref[:, c0:c0 + width]

    def step_rows(arr, t):
        return arr[t * nb:(t + 1) * nb]

    xr = proj(_XR0, LRU_WIDTH)
    ext = [sconv_ref[:, j * LRU_WIDTH:(j + 1) * LRU_WIDTH] for j in range(CONV_WIDTH - 1)]
    ext += [step_rows(xr, t) for t in range(steps)]
    xc_steps = []
    for t in range(steps):
        acc = conv_b_ref[...]
        for j in range(CONV_WIDTH):
            acc = acc + ext[t + j] * conv_w_rows[j][...]
        xc_steps.append(acc)
    for j in range(CONV_WIDTH - 1):
        conv_out_ref[:, j * LRU_WIDTH:(j + 1) * LRU_WIDTH] = ext[steps + j]
    xc = jnp.concatenate(xc_steps, axis=0)

    ra_half, ia_half = _gate_matmuls(xc, wa_ref, wx_ref)
    a, u = _gate_nonlin(xc, ra_half, ia_half, b_a_ref[...], b_x_ref[...], lam_ref[...])
    h = slru_ref[...]
    hs = []
    for t in range(steps):
        h = step_rows(a, t) * h + step_rows(u, t)
        hs.append(h)
    lru_out_ref[...] = h
    gate = proj(_GATE0, LRU_WIDTH)
    y_lru = _rms_norm(_gelu_tanh(gate) * jnp.concatenate(hs, axis=0), g_lru_ref[...])

    def table(ref):
        return jnp.concatenate([jnp.broadcast_to(ref[t:t + 1, :], (nb, LANES)) for t in range(steps)], axis=0)

    cos, sin = table(cos_ref), table(sin_ref)
    q = _rope(proj(_Q0, Q_DIM), cos * Q_SCALE, sin * Q_SCALE)
    k = _rope(proj(_K0, KV_DIM), cos, sin)
    v = proj(_V0, KV_DIM)

    k_out_ref[:, 0:win - steps, :] = kc_ref[:, steps:win, :]
    v_out_ref[:, 0:win - steps, :] = vc_ref[:, steps:win, :]
    for t in range(steps):
        k_out_ref[:, win - steps + t, :] = step_rows(k, t)
        v_out_ref[:, win - steps + t, :] = step_rows(v, t)

    sub = DEC_SUB
    n_sub = nb // sub
    n_cache = sub * win
    n_keys = n_cache + steps * sub
    qr = lax.broadcasted_iota(jnp.int32, (steps * sub, n_keys), 0)
    kc_col = lax.broadcasted_iota(jnp.int32, (steps * sub, n_keys), 1)
    q_step, q_seq = qr // sub, qr % sub
    cached = kc_col < n_cache
    new_col = kc_col - n_cache
    key_seq = jnp.where(cached, kc_col // win, new_col % sub)
    key_lo = jnp.where(cached, q_step + (win - WINDOW), 0)
    key_hi = jnp.where(cached, win - 1, q_step)
    key_idx = jnp.where(cached, kc_col % win, new_col // sub)
    mask = (key_seq == q_seq) & (key_idx >= key_lo) & (key_idx <= key_hi)

    def sub_rows(arr, j):
        return jnp.concatenate([arr[t * nb + j * sub:t * nb + (j + 1) * sub] for t in range(steps)], axis=0)

    units = []
    for j in range(n_sub):
        qj = sub_rows(q, j).astype(BF16)
        kall = jnp.concatenate([kc_ref[j * sub:(j + 1) * sub].reshape(n_cache, KV_DIM), sub_rows(k, j)], axis=0)
        vall = jnp.concatenate([vc_ref[j * sub:(j + 1) * sub].reshape(n_cache, KV_DIM), sub_rows(v, j)], axis=0)
        for kvh in range(N_KV_HEADS):
            kk = _stacked_halves(kall, kvh)
            vv = _stacked_halves(vall, kvh)
            for cc in range(2):
                c = 2 * kvh + cc
                units.append((_dot_nt(qj[:, c * LANES:(c + 1) * LANES], kk), vv, c))
    soft = [_sink_softmax_pair(s, mask, sinks_ref[2 * c] * LOG2E, sinks_ref[2 * c + 1] * LOG2E) for s, _, c in units]
    outs = [_pv_pair(p, vv, invs) for (p, invs), (_, vv, _) in zip(soft, units)]
    per_sub = N_HEADS // 2
    attn_sub = [jnp.concatenate(outs[j * per_sub:(j + 1) * per_sub], axis=1) for j in range(n_sub)]
    y_attn = jnp.concatenate(
        [attn_sub[j][t * sub:(t + 1) * sub] for t in range(steps) for j in range(n_sub)], axis=0)
    y_attn = _rms_norm(y_attn, g_attn_ref[...])

    y_mix = (_dot(y_lru.astype(BF16), w_out_ref[0:LRU_WIDTH, :])
             + _dot(y_attn.astype(BF16), w_out_ref[LRU_WIDTH:LRU_WIDTH + Q_DIM, :]) + b_out_ref[...])
    h1 = _layer_norm(ALPHA * x + y_mix, ln_g_ref[...], ln_b_ref[...])
    for t in range(steps):
        h1_ref[:, t * D_MODEL:(t + 1) * D_MODEL] = step_rows(h1, t)


def _decode_mixer(x2d, kc, vc, sconv, slru, tables, weights):
    nseq, width = x2d.shape
    steps = width // D_MODEL
    win = kc.shape[1]
    nb = DEC_BLOCK
    assert nseq % nb == 0 and nb % DEC_SUB == 0 and DEC_SUB % SUBLANES == 0 and steps <= win
    conv_w = (CONV_WIDTH - 1) * LRU_WIDTH
    in_specs = [
        pl.BlockSpec((nb, width), lambda i: (i, 0)),
        pl.BlockSpec((nb, win, KV_DIM), lambda i: (i, 0, 0)),
        pl.BlockSpec((nb, win, KV_DIM), lambda i: (i, 0, 0)),
        pl.BlockSpec((nb, conv_w), lambda i: (i, 0)),
        pl.BlockSpec((nb, LRU_WIDTH), lambda i: (i, 0)),
    ] + [_full_spec((steps, LANES))] * 2 + _mixer_weight_specs()
    out_shape = (
        jax.ShapeDtypeStruct((nseq, width), F32),
        jax.ShapeDtypeStruct((nseq, conv_w), F32),
        jax.ShapeDtypeStruct((nseq, LRU_WIDTH), F32),
        jax.ShapeDtypeStruct((nseq, win, KV_DIM), F32),
        jax.ShapeDtypeStruct((nseq, win, KV_DIM), F32),
    )
    out_specs = (
        pl.BlockSpec((nb, width), lambda i: (i, 0)),
        pl.BlockSpec((nb, conv_w), lambda i: (i, 0)),
        pl.BlockSpec((nb, LRU_WIDTH), lambda i: (i, 0)),
        pl.BlockSpec((nb, win, KV_DIM), lambda i: (i, 0, 0)),
        pl.BlockSpec((nb, win, KV_DIM), lambda i: (i, 0, 0)),
    )
    return pl.pallas_call(
        _decode_mixer_kernel,
        grid=(nseq // nb,),
        in_specs=in_specs,
        out_specs=out_specs,
        out_shape=out_shape,
        compiler_params=pltpu.CompilerParams(
            dimension_semantics=("arbitrary",), vmem_limit_bytes=VMEM_LIMIT_BYTES),
        name="decode_mixer",
    )(x2d, kc, vc, sconv, slru, *tables, *weights)


def _rope_tables(positions):
    inv = ROPE_THETA ** (-jnp.arange(HALF_DIM, dtype=F32) / HALF_DIM)
    ang = positions.astype(F32)[:, None] * inv[None, :]
    cos = jnp.cos(ang)
    sin = jnp.sin(ang)
    reps = LANES // HEAD_DIM
    cos_t = jnp.tile(jnp.concatenate([cos, cos], axis=1), (1, reps))
    sin_t = jnp.tile(jnp.concatenate([-sin, sin], axis=1), (1, reps))
    return cos_t, sin_t


def _block_diag_halves(w):
    per = LRU_BLOCKS // 2
    d = w.shape[-1]
    w4 = w.reshape(2, per, d, d)
    eye = jnp.eye(per, dtype=w.dtype)
    return jnp.einsum('paij,ab->paibj', w4, eye).reshape(2, per * d, per * d)


def kernel(x_prompt, x_sample, cache_k_win, cache_v_win, state_conv, state_lru, w_in, b_in, conv_w, conv_b,
           w_a, b_a, w_x, b_x, lru_lambda, sinks, g_lru, g_attn, w_out, b_out, ln1_g, ln1_b,
           w_gate, w_up, w_down, ln2_g, ln2_b):
    depth = w_in.shape[0]
    assert depth == 1
    l = 0
    bsz, seq, _ = x_prompt.shape
    nseq, steps, _ = x_sample.shape
    win = cache_k_win.shape[2]

    fields = dict(b_in=b_in, conv_b=conv_b, b_a=b_a, b_x=b_x, lam=lru_lambda, g_lru=g_lru, g_attn=g_attn,
                  b_out=b_out, ln1_g=ln1_g, ln1_b=ln1_b, ln2_g=ln2_g, ln2_b=ln2_b)
    fields.update({"conv_w%d" % j: conv_w[:, j] for j in range(CONV_WIDTH)})
    vec = jnp.concatenate([fields[name][l].reshape(-1) for name, _ in _VEC_FIELDS]).reshape(1, _VEC_LEN)

    mixer_w = (
        vec, sinks[l], w_in[l].astype(BF16),
        (0.5 * _block_diag_halves(w_a[l])).astype(BF16), (0.5 * _block_diag_halves(w_x[l])).astype(BF16),
        w_out[l].astype(BF16),
    )
    ffn_w = (w_gate[l].astype(BF16), w_up[l].astype(BF16), w_down[l].astype(BF16))

    y_p, conv_p, lru_p, k_p, v_p = _prompt_layer(
        x_prompt, _rope_tables(jnp.arange(seq, dtype=jnp.int32)), mixer_w, ffn_w)

    h1_s, conv_s, lru_s, k_s, v_s = _decode_mixer(
        x_sample.reshape(nseq, steps * D_MODEL),
        cache_k_win[l].reshape(nseq, win, KV_DIM), cache_v_win[l].reshape(nseq, win, KV_DIM),
        state_conv[l].reshape(nseq, (CONV_WIDTH - 1) * LRU_WIDTH), state_lru[l],
        _rope_tables(PAST_LEN + jnp.arange(steps, dtype=jnp.int32)), mixer_w)
    y_s = _ffn(h1_s, vec, *ffn_w, tile=nseq).reshape(nseq, steps, D_MODEL)

    keep = k_p.shape[1]
    return (
        y_p, y_s,
        conv_p[None], lru_p.reshape(1, bsz, LRU_WIDTH),
        k_p.reshape(1, bsz, keep, N_KV_HEADS, HEAD_DIM), v_p.reshape(1, bsz, keep, N_KV_HEADS, HEAD_DIM),
        conv_s.reshape(1, nseq, CONV_WIDTH - 1, LRU_WIDTH), lru_s[None],
        k_s.reshape(1, nseq, win, N_KV_HEADS, HEAD_DIM), v_s.reshape(1, nseq, win, N_KV_HEADS, HEAD_DIM),
    )
```

```python
import functools

import jax
import jax.numpy as jnp
from jax import lax
from jax.experimental import pallas as pl
from jax.experimental.pallas import tpu as pltpu

D_MODEL = 1024
LRU_WIDTH = 512
LRU_BLOCKS = 8
CONV_WIDTH = 4
LRU_C = 8.0
N_HEADS = 8
N_KV_HEADS = 2
HEAD_DIM = 64
HALF_DIM = HEAD_DIM // 2
Q_DIM = N_HEADS * HEAD_DIM
KV_DIM = N_KV_HEADS * HEAD_DIM
IN_DIM = 2 * LRU_WIDTH + Q_DIM + 2 * KV_DIM
WINDOW = 128
BLOCK_Q = 128
ROPE_THETA = 10000.0
ATTN_SCALE = HEAD_DIM ** -0.5
D_FF = 2816
ALPHA = 2.0 ** 0.25
PAST_LEN = 16384
LOG2E = 1.4426950408889634
Q_SCALE = ATTN_SCALE * LOG2E

LANES = 128
SUBLANES = 8
VMEM_LIMIT_BYTES = 56 * 1024 * 1024

_XR0, _GATE0, _Q0, _K0, _V0 = 0, LRU_WIDTH, 2 * LRU_WIDTH, 2 * LRU_WIDTH + Q_DIM, 2 * LRU_WIDTH + Q_DIM + KV_DIM

PROMPT_TILE = 512
FFN_CHUNK = 256
FFN_HEAD_CHUNKS = 3
FFN_DOWN_GROUP = 1
DEC_BLOCK = 32
DEC_SUB = 8

BF16 = jnp.bfloat16
F32 = jnp.float32

_VEC_FIELDS = (
    ("b_in", IN_DIM), ("conv_w0", LRU_WIDTH), ("conv_w1", LRU_WIDTH), ("conv_w2", LRU_WIDTH), ("conv_w3", LRU_WIDTH),
    ("conv_b", LRU_WIDTH), ("b_a", LRU_WIDTH), ("b_x", LRU_WIDTH), ("lam", LRU_WIDTH),
    ("g_lru", LRU_WIDTH), ("g_attn", Q_DIM), ("b_out", D_MODEL),
    ("ln1_g", D_MODEL), ("ln1_b", D_MODEL), ("ln2_g", D_MODEL), ("ln2_b", D_MODEL),
)
_VEC_LAYOUT = {}
_VEC_LEN = 0
for _name, _n in _VEC_FIELDS:
    assert _n % 128 == 0
    _VEC_LAYOUT[_name] = (_VEC_LEN, _n)
    _VEC_LEN += _n


def _vec_views(vec_ref):
    return {name: vec_ref.at[:, off:off + n] for name, (off, n) in _VEC_LAYOUT.items()}


def _dot(a, b):
    return jnp.dot(a, b, preferred_element_type=F32)


def _dot_nt(a, b):
    return lax.dot_general(a, b, (((1,), (1,)), ((), ())), preferred_element_type=F32)


def _layer_norm(x, g, b, eps=1e-5):
    mu = jnp.mean(x, axis=-1, keepdims=True)
    xc = x - mu
    var = jnp.mean(xc * xc, axis=-1, keepdims=True)
    return xc * lax.rsqrt(var + eps) * g + b


def _rms_norm(x, g, eps=1e-6):
    return x * lax.rsqrt(jnp.mean(x * x, axis=-1, keepdims=True) + eps) * g


def _gelu_tanh(x):
    c = 0.7978845608028654
    return 0.5 * x * (1.0 + jnp.tanh(c * (x + 0.044715 * (x * x * x))))


def _softplus(x):
    return jnp.maximum(x, 0.0) + jnp.log1p(jnp.exp(-jnp.abs(x)))


def _swap_halves(x):
    w = x.shape[-1]
    lane = lax.broadcasted_iota(jnp.int32, (1, w), 1)
    first_half = (lane % HEAD_DIM) < HALF_DIM
    return jnp.where(first_half, pltpu.roll(x, w - HALF_DIM, 1), pltpu.roll(x, HALF_DIM, 1))


def _rope(x, cos, sin_signed):
    reps = x.shape[-1] // LANES
    if reps > 1:
        cos = jnp.concatenate([cos] * reps, axis=1)
        sin_signed = jnp.concatenate([sin_signed] * reps, axis=1)
    return x * cos + _swap_halves(x) * sin_signed


def _gate_matmuls(xc, wa_ref, wx_ref):
    xcb = xc.astype(BF16)
    half = LRU_WIDTH // 2
    lo, hi = xcb[:, :half], xcb[:, half:]
    ra = jnp.concatenate([_dot(lo, wa_ref[0]), _dot(hi, wa_ref[1])], axis=1)
    ia = jnp.concatenate([_dot(lo, wx_ref[0]), _dot(hi, wx_ref[1])], axis=1)
    return ra, ia


def _gate_nonlin(xc, ra_half, ia_half, b_a, b_x, lam):
    c_half = (-0.5 * LRU_C) * _softplus(-lam)
    t_r = jnp.tanh(ra_half + 0.5 * b_a)
    neg_log_a = (-c_half) * t_r - c_half
    a = jnp.exp2(neg_log_a * (-LOG2E))
    y = jnp.tanh(neg_log_a) * (a * a + 1.0)
    mult = jnp.where(y > 0.0, y * lax.rsqrt(y), 0.0)
    t_i = jnp.tanh(ia_half + 0.5 * b_x)
    hx = 0.5 * xc
    u = mult * (hx * t_i + hx)
    return a, u


def _lane_halves(x, kvh):
    lane = lax.broadcasted_iota(jnp.int32, (1, x.shape[1]), 1)
    low = lane < HEAD_DIM
    rolled = pltpu.roll(x, HEAD_DIM, 1)
    if kvh == 0:
        return jnp.where(low, x, 0.0), jnp.where(low, 0.0, rolled)
    return jnp.where(low, rolled, 0.0), jnp.where(low, 0.0, x)


def _stacked_halves(x, kvh):
    return jnp.concatenate(_lane_halves(x, kvh), axis=0).astype(BF16)


def _sink_softmax_pair(scores, mask, sink_lo, sink_hi):
    s_len = mask.shape[1]
    ps, invs = [], []
    for half, sink in ((0, sink_lo), (1, sink_hi)):
        sh = jnp.where(mask, scores[:, half * s_len:(half + 1) * s_len], -jnp.inf)
        m = jnp.maximum(jnp.max(sh, axis=-1, keepdims=True), sink)
        p = jnp.exp2(sh - m)
        denom = jnp.sum(p, axis=-1, keepdims=True) + jnp.exp2(sink - m)
        ps.append(p.astype(BF16))
        invs.append(1.0 / denom)
    return jnp.concatenate(ps, axis=1), invs


def _pv_pair(probs, vv, invs):
    out = _dot(probs, vv)
    lane = lax.broadcasted_iota(jnp.int32, (1, LANES), 1)
    return out * jnp.where(lane < HEAD_DIM, invs[0], invs[1])


def _scan_groups(a, u, h0):
    rows, width = a.shape
    groups = rows // SUBLANES
    a3 = a.reshape(groups, SUBLANES, width)
    u3 = u.reshape(groups, SUBLANES, width)
    row = lax.broadcasted_iota(jnp.int32, (1, SUBLANES, 1), 1)
    d = 1
    while d < SUBLANES:
        valid = row >= d
        u3 = u3 + a3 * jnp.where(valid, pltpu.roll(u3, d, 1), 0.0)
        a3 = a3 * jnp.where(valid, pltpu.roll(a3, d, 1), 1.0)
        d *= 2
    h = h0
    hs = []
    for g in range(groups):
        hg = u3[g] + a3[g] * h
        h = hg[SUBLANES - 1:SUBLANES, :]
        hs.append(hg)
    return jnp.concatenate(hs, axis=0), h


class _FfnStream:
    def __init__(self, lhs, acc, wg_ref, wu_ref, wd_ref, first, last):
        self.lhs, self.acc = lhs, acc
        self.wg_ref, self.wu_ref, self.wd_ref = wg_ref, wu_ref, wd_ref
        self.next, self.last = first, last
        self.pending = []
        self.hids = []
        self.hid0 = first

    def _activate(self):
        gt, up = self.pending.pop(0)
        self.hids.append((gt * jax.nn.sigmoid(gt) * up).astype(BF16))
        if len(self.hids) == FFN_DOWN_GROUP:
            self._down()

    def _down(self):
        rows = slice(self.hid0 * FFN_CHUNK, (self.hid0 + len(self.hids)) * FFN_CHUNK)
        part = _dot(jnp.concatenate(self.hids, axis=1), self.wd_ref[rows, :])
        self.acc = part if self.acc is None else self.acc + part
        self.hid0 += len(self.hids)
        self.hids = []

    def emit(self, count):
        for _ in range(count):
            if self.next < self.last:
                cols = slice(self.next * FFN_CHUNK, (self.next + 1) * FFN_CHUNK)
                self.pending.append((_dot(self.lhs, self.wg_ref[:, cols]), _dot(self.lhs, self.wu_ref[:, cols])))
                self.next += 1
                if len(self.pending) > 1:
                    self._activate()

    def finish(self):
        self.emit(self.last - self.next)
        while self.pending:
            self._activate()
        if self.hids:
            self._down()
        return self.acc


def _prompt_layer_kernel(x_ref, cos_ref, sin_ref, vec_ref, sinks_ref,
                         w_in_ref, wa_ref, wx_ref, w_out_ref, wg_ref, wu_ref, wd_ref,
                         y_ref, conv_out_ref, lru_out_ref, k_out_ref, v_out_ref,
                         xr_ext, h_carry, k_prev, v_prev, h1_scr, hb_scr, acc_scr, *, n_t, n_tiles):
    vec = _vec_views(vec_ref)
    b_in_ref, conv_b_ref, b_a_ref, b_x_ref, lam_ref = (vec[n] for n in ("b_in", "conv_b", "b_a", "b_x", "lam"))
    g_lru_ref, g_attn_ref, b_out_ref = (vec[n] for n in ("g_lru", "g_attn", "b_out"))
    ln1_g_ref, ln1_b_ref, ln2_g_ref, ln2_b_ref = (vec[n] for n in ("ln1_g", "ln1_b", "ln2_g", "ln2_b"))
    conv_w_rows = [vec["conv_w%d" % j] for j in range(CONV_WIDTH)]
    g_idx = pl.program_id(0)
    t_idx = lax.rem(g_idx, n_t)
    tile = x_ref.shape[0]
    hist = SUBLANES

    @pl.when(g_idx == 0)
    def _():
        h1_scr[...] = jnp.zeros_like(h1_scr)
        hb_scr[...] = jnp.zeros_like(hb_scr)
        acc_scr[...] = jnp.zeros_like(acc_scr)

    @pl.when(t_idx == 0)
    def _():
        xr_ext[0:hist, :] = jnp.zeros((hist, LRU_WIDTH), F32)
        h_carry[...] = jnp.zeros_like(h_carry)
        k_prev[...] = jnp.zeros_like(k_prev)
        v_prev[...] = jnp.zeros_like(v_prev)

    n_chunks = D_FF // FFN_CHUNK
    prev_ffn = _FfnStream(hb_scr[...], acc_scr[...], wg_ref, wu_ref, wd_ref, FFN_HEAD_CHUNKS, n_chunks)
    ffn_chunks = prev_ffn.emit

    x = x_ref[...]
    xb = x.astype(BF16)

    def proj(c0, width):
        return _dot(xb, w_in_ref[:, c0:c0 + width]) + b_in_ref[:, c0:c0 + width]

    xr = proj(_XR0, LRU_WIDTH)
    xr_ext[hist:hist + tile, :] = xr
    q_raw = proj(_Q0, Q_DIM)
    k_raw = proj(_K0, KV_DIM)
    v = proj(_V0, KV_DIM)
    gate = proj(_GATE0, LRU_WIDTH)

    ffn_chunks(1)
    xc = conv_b_ref[...] + xr * conv_w_rows[CONV_WIDTH - 1][...]
    for j in range(CONV_WIDTH - 1):
        shift = CONV_WIDTH - 1 - j
        xc = xc + xr_ext[hist - shift:hist - shift + tile, :] * conv_w_rows[j][...]
    xr_ext[0:hist, :] = xr_ext[tile:tile + hist, :]

    ra_half, ia_half = _gate_matmuls(xc, wa_ref, wx_ref)

    ffn_chunks(1)
    cos, sin = cos_ref[...], sin_ref[...]
    qb = _rope(q_raw, cos * Q_SCALE, sin * Q_SCALE).astype(BF16)
    k = _rope(k_raw, cos, sin)

    qi = lax.broadcasted_iota(jnp.int32, (BLOCK_Q, 2 * BLOCK_Q), 0)
    kj = lax.broadcasted_iota(jnp.int32, (BLOCK_Q, 2 * BLOCK_Q), 1)
    band = (kj >= qi) & (kj <= qi + WINDOW)
    first_lo = jnp.where(t_idx == 0, BLOCK_Q, 0)
    band_first = band & (kj >= first_lo)

    n_blocks = tile // BLOCK_Q
    units = []
    for n in range(n_blocks):
        rows = slice(n * BLOCK_Q, (n + 1) * BLOCK_Q)
        if n == 0:
            k2 = jnp.concatenate([k_prev[...], k[rows]], axis=0)
            v2 = jnp.concatenate([v_prev[...], v[rows]], axis=0)
            mask = band_first
        else:
            k2 = k[(n - 1) * BLOCK_Q:(n + 1) * BLOCK_Q]
            v2 = v[(n - 1) * BLOCK_Q:(n + 1) * BLOCK_Q]
            mask = band
        for kvh in range(N_KV_HEADS):
            kk = _stacked_halves(k2, kvh)
            vv = _stacked_halves(v2, kvh)
            for cc in range(2):
                c = 2 * kvh + cc
                units.append((_dot_nt(qb[rows, c * LANES:(c + 1) * LANES], kk), vv, mask, c))
    k_prev[...] = k[tile - BLOCK_Q:tile]
    v_prev[...] = v[tile - BLOCK_Q:tile]

    ffn_chunks(1)
    a, u = _gate_nonlin(xc, ra_half, ia_half, b_a_ref[...], b_x_ref[...], lam_ref[...])

    ffn_chunks(1)
    soft = [_sink_softmax_pair(s, mask, sinks_ref[2 * c] * LOG2E, sinks_ref[2 * c + 1] * LOG2E)
            for s, _, mask, c in units]

    outs = [_pv_pair(p, vv, invs) for (p, invs), (_, vv, _, _) in zip(soft, units)]

    ffn_chunks(2)
    h_all, h_last = _scan_groups(a, u, h_carry[...])
    h_carry[...] = h_last

    y_lru = _rms_norm(_gelu_tanh(gate) * h_all, g_lru_ref[...])
    per_block = N_HEADS // 2
    y_attn = jnp.concatenate(
        [jnp.concatenate(outs[n * per_block:(n + 1) * per_block], axis=1) for n in range(n_blocks)], axis=0)
    y_attn = _rms_norm(y_attn, g_attn_ref[...])

    y_mix = (_dot(y_lru.astype(BF16), w_out_ref[0:LRU_WIDTH, :])
             + _dot(y_attn.astype(BF16), w_out_ref[LRU_WIDTH:LRU_WIDTH + Q_DIM, :]) + b_out_ref[...])
    h1 = _layer_norm(ALPHA * x + y_mix, ln1_g_ref[...], ln1_b_ref[...])
    acc_prev = prev_ffn.finish()
    h_prev = h1_scr[...]
    hb = h1.astype(BF16)
    h1_scr[...] = h1
    hb_scr[...] = hb

    head_ffn = _FfnStream(hb, None, wg_ref, wu_ref, wd_ref, 0, FFN_HEAD_CHUNKS)
    head_ffn.emit(1)
    y_ref[...] = _layer_norm(ALPHA * h_prev + acc_prev, ln2_g_ref[...], ln2_b_ref[...])
    acc_scr[...] = head_ffn.finish()

    @pl.when((t_idx == n_t - 1) & (g_idx < n_tiles))
    def _():
        conv_out_ref[...] = xr_ext[hist - (CONV_WIDTH - 1):hist, :]
        lru_out_ref[...] = h_last
        k_out_ref[...] = k[tile - WINDOW:tile]
        v_out_ref[...] = v[tile - WINDOW:tile]


def _full_spec(shape):
    zeros = (0,) * len(shape)
    return pl.BlockSpec(shape, lambda *_: zeros, pipeline_mode=pl.Buffered(1))


def _vec_spec():
    return _full_spec((1, _VEC_LEN))


def _mixer_weight_specs():
    return [
        _vec_spec(), pl.BlockSpec(memory_space=pltpu.SMEM),
        _full_spec((D_MODEL, IN_DIM)),
        _full_spec((2, LRU_WIDTH // 2, LRU_WIDTH // 2)), _full_spec((2, LRU_WIDTH // 2, LRU_WIDTH // 2)),
        _full_spec((LRU_WIDTH + Q_DIM, D_MODEL)),
    ]


def _ffn_weight_specs():
    return [_full_spec((D_MODEL, D_FF)), _full_spec((D_MODEL, D_FF)), _full_spec((D_FF, D_MODEL))]


def _prompt_layer(x, tables, mixer_w, ffn_w):
    bsz, seq, _ = x.shape
    tile = PROMPT_TILE
    assert seq % tile == 0 and tile % BLOCK_Q == 0 and tile >= WINDOW and D_FF % FFN_CHUNK == 0
    n_t = seq // tile
    n_tiles = bsz * n_t
    keep = min(WINDOW, seq)

    def cur(g):
        gi = jnp.minimum(g, n_tiles - 1)
        return lax.div(gi, n_t), lax.rem(gi, n_t)

    def prev(g):
        go = jnp.maximum(g - 1, 0)
        return lax.div(go, n_t), lax.rem(go, n_t)

    table_spec = pl.BlockSpec((tile, LANES), lambda g: (cur(g)[1], 0))
    in_specs = [pl.BlockSpec((None, tile, D_MODEL), lambda g: (*cur(g), 0))] + [table_spec] * 2
    in_specs += _mixer_weight_specs() + _ffn_weight_specs()
    out_shape = (
        jax.ShapeDtypeStruct((bsz, seq, D_MODEL), F32),
        jax.ShapeDtypeStruct((bsz, CONV_WIDTH - 1, LRU_WIDTH), F32),
        jax.ShapeDtypeStruct((bsz, 1, LRU_WIDTH), F32),
        jax.ShapeDtypeStruct((bsz, keep, KV_DIM), F32),
        jax.ShapeDtypeStruct((bsz, keep, KV_DIM), F32),
    )
    out_specs = (
        pl.BlockSpec((None, tile, D_MODEL), lambda g: (*prev(g), 0)),
        pl.BlockSpec((None, CONV_WIDTH - 1, LRU_WIDTH), lambda g: (cur(g)[0], 0, 0)),
        pl.BlockSpec((None, 1, LRU_WIDTH), lambda g: (cur(g)[0], 0, 0)),
        pl.BlockSpec((None, keep, KV_DIM), lambda g: (cur(g)[0], 0, 0)),
        pl.BlockSpec((None, keep, KV_DIM), lambda g: (cur(g)[0], 0, 0)),
    )
    scratch = [
        pltpu.VMEM((tile + SUBLANES, LRU_WIDTH), F32),
        pltpu.VMEM((1, LRU_WIDTH), F32),
        pltpu.VMEM((BLOCK_Q, KV_DIM), F32),
        pltpu.VMEM((BLOCK_Q, KV_DIM), F32),
        pltpu.VMEM((tile, D_MODEL), F32),
        pltpu.VMEM((tile, D_MODEL), BF16),
        pltpu.VMEM((tile, D_MODEL), F32),
    ]
    return pl.pallas_call(
        functools.partial(_prompt_layer_kernel, n_t=n_t, n_tiles=n_tiles),
        grid=(n_tiles + 1,),
        in_specs=in_specs,
        out_specs=out_specs,
        out_shape=out_shape,
        scratch_shapes=scratch,
        compiler_params=pltpu.CompilerParams(
            dimension_semantics=("arbitrary",), vmem_limit_bytes=VMEM_LIMIT_BYTES),
        name="prompt_layer",
    )(x, *tables, *mixer_w, *ffn_w)


def _ffn_kernel(h_ref, vec_ref, wg_ref, wu_ref, wd_ref, o_ref):
    vec = _vec_views(vec_ref)
    ln_g_ref, ln_b_ref = vec["ln2_g"], vec["ln2_b"]
    slabs = h_ref.shape[1] // D_MODEL
    h = jnp.concatenate([h_ref[:, s * D_MODEL:(s + 1) * D_MODEL] for s in range(slabs)], axis=0)
    acc = _FfnStream(h.astype(BF16), None, wg_ref, wu_ref, wd_ref, 0, D_FF // FFN_CHUNK).finish()
    out = _layer_norm(ALPHA * h + acc, ln_g_ref[...], ln_b_ref[...])
    rows = h_ref.shape[0]
    for s in range(slabs):
        o_ref[:, s * D_MODEL:(s + 1) * D_MODEL] = out[s * rows:(s + 1) * rows]


def _ffn(h, vec, wg, wu, wd, tile):
    rows, width = h.shape
    assert rows % tile == 0 and width % D_MODEL == 0 and D_FF % FFN_CHUNK == 0
    return pl.pallas_call(
        _ffn_kernel,
        grid=(rows // tile,),
        in_specs=[pl.BlockSpec((tile, width), lambda i: (i, 0)), _vec_spec()] + _ffn_weight_specs(),
        out_specs=pl.BlockSpec((tile, width), lambda i: (i, 0)),
        out_shape=jax.ShapeDtypeStruct((rows, width), F32),
        compiler_params=pltpu.CompilerParams(
            dimension_semantics=("arbitrary",), vmem_limit_bytes=VMEM_LIMIT_BYTES),
        name="ffn",
    )(h, vec, wg, wu, wd)


def _decode_mixer_kernel(x_ref, kc_ref, vc_ref, sconv_ref, slru_ref, cos_ref, sin_ref, vec_ref, sinks_ref,
                         w_in_ref, wa_ref, wx_ref, w_out_ref,
                         h1_ref, conv_out_ref, lru_out_ref, k_out_ref, v_out_ref):
    vec = _vec_views(vec_ref)
    b_in_ref, conv_b_ref, b_a_ref, b_x_ref, lam_ref = (vec[n] for n in ("b_in", "conv_b", "b_a", "b_x", "lam"))
    g_lru_ref, g_attn_ref, b_out_ref, ln_g_ref, ln_b_ref = (
        vec[n] for n in ("g_lru", "g_attn", "b_out", "ln1_g", "ln1_b"))
    conv_w_rows = [vec["conv_w%d" % j] for j in range(CONV_WIDTH)]
    nb = x_ref.shape[0]
    steps = x_ref.shape[1] // D_MODEL
    win = kc_ref.shape[1]

    x = jnp.concatenate([x_ref[:, t * D_MODEL:(t + 1) * D_MODEL] for t in range(steps)], axis=0)
    xb = x.astype(BF16)

    def proj(c0, width):
        return _dot(xb, w_in_ref[:, c0:c0 + width]) + b_in_ref[:, c0:c0 + width]

    def step_rows(arr, t):
        return arr[t * nb:(t + 1) * nb]

    xr = proj(_XR0, LRU_WIDTH)
    ext = [sconv_ref[j] for j in range(CONV_WIDTH - 1)]
    ext += [step_rows(xr, t) for t in range(steps)]
    xc_steps = []
    for t in range(steps):
        acc = conv_b_ref[...]
        for j in range(CONV_WIDTH):
            acc = acc + ext[t + j] * conv_w_rows[j][...]
        xc_steps.append(acc)
    for j in range(CONV_WIDTH - 1):
        conv_out_ref[j] = ext[steps + j]
    xc = jnp.concatenate(xc_steps, axis=0)

    ra_half, ia_half = _gate_matmuls(xc, wa_ref, wx_ref)
    a, u = _gate_nonlin(xc, ra_half, ia_half, b_a_ref[...], b_x_ref[...], lam_ref[...])
    h = slru_ref[...]
    hs = []
    for t in range(steps):
        h = step_rows(a, t) * h + step_rows(u, t)
        hs.append(h)
    lru_out_ref[...] = h
    gate = proj(_GATE0, LRU_WIDTH)
    y_lru = _rms_norm(_gelu_tanh(gate) * jnp.concatenate(hs, axis=0), g_lru_ref[...])

    def table(ref):
        return jnp.concatenate([jnp.broadcast_to(ref[t:t + 1, :], (nb, LANES)) for t in range(steps)], axis=0)

    cos, sin = table(cos_ref), table(sin_ref)
    q = _rope(proj(_Q0, Q_DIM), cos * Q_SCALE, sin * Q_SCALE)
    k = _rope(proj(_K0, KV_DIM), cos, sin)
    v = proj(_V0, KV_DIM)

    k_out_ref[:, 0:win - steps, :] = kc_ref[:, steps:win, :]
    v_out_ref[:, 0:win - steps, :] = vc_ref[:, steps:win, :]
    for t in range(steps):
        k_out_ref[:, win - steps + t, :] = step_rows(k, t)
        v_out_ref[:, win - steps + t, :] = step_rows(v, t)

    sub = DEC_SUB
    n_sub = nb // sub
    n_cache = sub * win
    n_keys = n_cache + steps * sub
    qr = lax.broadcasted_iota(jnp.int32, (steps * sub, n_keys), 0)
    kc_col = lax.broadcasted_iota(jnp.int32, (steps * sub, n_keys), 1)
    q_step, q_seq = qr // sub, qr % sub
    cached = kc_col < n_cache
    new_col = kc_col - n_cache
    key_seq = jnp.where(cached, kc_col // win, new_col % sub)
    key_lo = jnp.where(cached, q_step + (win - WINDOW), 0)
    key_hi = jnp.where(cached, win - 1, q_step)
    key_idx = jnp.where(cached, kc_col % win, new_col // sub)
    mask = (key_seq == q_seq) & (key_idx >= key_lo) & (key_idx <= key_hi)

    def sub_rows(arr, j):
        return jnp.concatenate([arr[t * nb + j * sub:t * nb + (j + 1) * sub] for t in range(steps)], axis=0)

    units = []
    for j in range(n_sub):
        qj = sub_rows(q, j).astype(BF16)
        kall = jnp.concatenate([kc_ref[j * sub:(j + 1) * sub].reshape(n_cache, KV_DIM), sub_rows(k, j)], axis=0)
        vall = jnp.concatenate([vc_ref[j * sub:(j + 1) * sub].reshape(n_cache, KV_DIM), sub_rows(v, j)], axis=0)
        for kvh in range(N_KV_HEADS):
            kk = _stacked_halves(kall, kvh)
            vv = _stacked_halves(vall, kvh)
            for cc in range(2):
                c = 2 * kvh + cc
                units.append((_dot_nt(qj[:, c * LANES:(c + 1) * LANES], kk), vv, c))
    soft = [_sink_softmax_pair(s, mask, sinks_ref[2 * c] * LOG2E, sinks_ref[2 * c + 1] * LOG2E) for s, _, c in units]
    outs = [_pv_pair(p, vv, invs) for (p, invs), (_, vv, _) in zip(soft, units)]
    per_sub = N_HEADS // 2
    attn_sub = [jnp.concatenate(outs[j * per_sub:(j + 1) * per_sub], axis=1) for j in range(n_sub)]
    y_attn = jnp.concatenate(
        [attn_sub[j][t * sub:(t + 1) * sub] for t in range(steps) for j in range(n_sub)], axis=0)
    y_attn = _rms_norm(y_attn, g_attn_ref[...])

    y_mix = (_dot(y_lru.astype(BF16), w_out_ref[0:LRU_WIDTH, :])
             + _dot(y_attn.astype(BF16), w_out_ref[LRU_WIDTH:LRU_WIDTH + Q_DIM, :]) + b_out_ref[...])
    h1 = _layer_norm(ALPHA * x + y_mix, ln_g_ref[...], ln_b_ref[...])
    for t in range(steps):
        h1_ref[:, t * D_MODEL:(t + 1) * D_MODEL] = step_rows(h1, t)


def _decode_mixer(x2d, kc, vc, sconv, slru, tables, weights):
    nseq, width = x2d.shape
    steps = width // D_MODEL
    win = kc.shape[1]
    nb = DEC_BLOCK
    assert nseq % nb == 0 and nb % DEC_SUB == 0 and DEC_SUB % SUBLANES == 0 and steps <= win
    conv_spec = pl.BlockSpec((CONV_WIDTH - 1, nb, LRU_WIDTH), lambda i: (0, i, 0))
    in_specs = [
        pl.BlockSpec((nb, width), lambda i: (i, 0)),
        pl.BlockSpec((nb, win, KV_DIM), lambda i: (i, 0, 0)),
        pl.BlockSpec((nb, win, KV_DIM), lambda i: (i, 0, 0)),
        conv_spec,
        pl.BlockSpec((nb, LRU_WIDTH), lambda i: (i, 0)),
    ] + [_full_spec((steps, LANES))] * 2 + _mixer_weight_specs()
    out_shape = (
        jax.ShapeDtypeStruct((nseq, width), F32),
        jax.ShapeDtypeStruct((CONV_WIDTH - 1, nseq, LRU_WIDTH), F32),
        jax.ShapeDtypeStruct((nseq, LRU_WIDTH), F32),
        jax.ShapeDtypeStruct((nseq, win, KV_DIM), F32),
        jax.ShapeDtypeStruct((nseq, win, KV_DIM), F32),
    )
    out_specs = (
        pl.BlockSpec((nb, width), lambda i: (i, 0)),
        conv_spec,
        pl.BlockSpec((nb, LRU_WIDTH), lambda i: (i, 0)),
        pl.BlockSpec((nb, win, KV_DIM), lambda i: (i, 0, 0)),
        pl.BlockSpec((nb, win, KV_DIM), lambda i: (i, 0, 0)),
    )
    return pl.pallas_call(
        _decode_mixer_kernel,
        grid=(nseq // nb,),
        in_specs=in_specs,
        out_specs=out_specs,
        out_shape=out_shape,
        compiler_params=pltpu.CompilerParams(
            dimension_semantics=("arbitrary",), vmem_limit_bytes=VMEM_LIMIT_BYTES),
        name="decode_mixer",
    )(x2d, kc, vc, sconv, slru, *tables, *weights)


def _rope_tables(positions):
    lane = jnp.arange(LANES, dtype=jnp.int32)
    inv = ROPE_THETA ** (-(lane % HALF_DIM).astype(F32) / HALF_DIM)
    ang = positions.astype(F32)[:, None] * inv[None, :]
    sin = jnp.sin(ang)
    return jnp.cos(ang), jnp.where((lane % HEAD_DIM < HALF_DIM)[None, :], -sin, sin)


def _block_diag_halves(w):
    per = LRU_BLOCKS // 2
    d = w.shape[-1]
    w4 = w.reshape(2, per, d, d)
    eye = jnp.eye(per, dtype=w.dtype)
    return jnp.einsum('paij,ab->paibj', w4, eye).reshape(2, per * d, per * d)


def kernel(x_prompt, x_sample, cache_k_win, cache_v_win, state_conv, state_lru, w_in, b_in, conv_w, conv_b,
           w_a, b_a, w_x, b_x, lru_lambda, sinks, g_lru, g_attn, w_out, b_out, ln1_g, ln1_b,
           w_gate, w_up, w_down, ln2_g, ln2_b):
    depth = w_in.shape[0]
    assert depth == 1
    l = 0
    bsz, seq, _ = x_prompt.shape
    nseq, steps, _ = x_sample.shape
    win = cache_k_win.shape[2]

    fields = dict(b_in=b_in, conv_b=conv_b, b_a=b_a, b_x=b_x, lam=lru_lambda, g_lru=g_lru, g_attn=g_attn,
                  b_out=b_out, ln1_g=ln1_g, ln1_b=ln1_b, ln2_g=ln2_g, ln2_b=ln2_b)
    fields.update({"conv_w%d" % j: conv_w[:, j] for j in range(CONV_WIDTH)})
    vec = jnp.concatenate([fields[name][l].reshape(-1) for name, _ in _VEC_FIELDS]).reshape(1, _VEC_LEN)

    mixer_w = (
        vec, sinks[l], w_in[l].astype(BF16),
        (0.5 * _block_diag_halves(w_a[l])).astype(BF16), (0.5 * _block_diag_halves(w_x[l])).astype(BF16),
        w_out[l].astype(BF16),
    )
    ffn_w = (w_gate[l].astype(BF16), w_up[l].astype(BF16), w_down[l].astype(BF16))

    y_p, conv_p, lru_p, k_p, v_p = _prompt_layer(
        x_prompt, _rope_tables(jnp.arange(seq, dtype=jnp.int32)), mixer_w, ffn_w)

    h1_s, conv_s, lru_s, k_s, v_s = _decode_mixer(
        x_sample.reshape(nseq, steps * D_MODEL),
        cache_k_win[l].reshape(nseq, win, KV_DIM), cache_v_win[l].reshape(nseq, win, KV_DIM),
        jnp.transpose(state_conv[l], (1, 0, 2)), state_lru[l],
        _rope_tables(PAST_LEN + jnp.arange(steps, dtype=jnp.int32)), mixer_w)
    y_s = _ffn(h1_s, vec, *ffn_w, tile=nseq).reshape(nseq, steps, D_MODEL)

    keep = k_p.shape[1]
    return (
        y_p, y_s,
        conv_p[None], lru_p.reshape(1, bsz, LRU_WIDTH),
        k_p.reshape(1, bsz, keep, N_KV_HEADS, HEAD_DIM), v_p.reshape(1, bsz, keep, N_KV_HEADS, HEAD_DIM),
        jnp.transpose(conv_s, (1, 0, 2))[None], lru_s[None],
        k_s.reshape(1, nseq, win, N_KV_HEADS, HEAD_DIM), v_s.reshape(1, nseq, win, N_KV_HEADS, HEAD_DIM),
    )
```

```python
import functools

import jax
import jax.numpy as jnp
from jax import lax
from jax.experimental import pallas as pl
from jax.experimental.pallas import tpu as pltpu

D_MODEL = 1024
LRU_WIDTH = 512
LRU_BLOCKS = 8
CONV_WIDTH = 4
LRU_C = 8.0
N_HEADS = 8
N_KV_HEADS = 2
HEAD_DIM = 64
HALF_DIM = HEAD_DIM // 2
Q_DIM = N_HEADS * HEAD_DIM
KV_DIM = N_KV_HEADS * HEAD_DIM
IN_DIM = 2 * LRU_WIDTH + Q_DIM + 2 * KV_DIM
WINDOW = 128
BLOCK_Q = 128
ROPE_THETA = 10000.0
ATTN_SCALE = HEAD_DIM ** -0.5
D_FF = 2816
ALPHA = 2.0 ** 0.25
PAST_LEN = 16384
LOG2E = 1.4426950408889634
Q_SCALE = ATTN_SCALE * LOG2E

LANES = 128
SUBLANES = 8
VMEM_LIMIT_BYTES = 56 * 1024 * 1024

_XR0, _GATE0, _Q0, _K0, _V0 = 0, LRU_WIDTH, 2 * LRU_WIDTH, 2 * LRU_WIDTH + Q_DIM, 2 * LRU_WIDTH + Q_DIM + KV_DIM

PROMPT_TILE = 512
FFN_CHUNK = 256
FFN_HEAD_CHUNKS = 3
DEC_BLOCK = 32
DEC_SUB = 8

BF16 = jnp.bfloat16
F32 = jnp.float32

_VEC_FIELDS = (
    ("b_in", IN_DIM), ("conv_w0", LRU_WIDTH), ("conv_w1", LRU_WIDTH), ("conv_w2", LRU_WIDTH), ("conv_w3", LRU_WIDTH),
    ("conv_b", LRU_WIDTH), ("b_a", LRU_WIDTH), ("b_x", LRU_WIDTH), ("lam", LRU_WIDTH),
    ("g_lru", LRU_WIDTH), ("g_attn", Q_DIM), ("b_out", D_MODEL),
    ("ln1_g", D_MODEL), ("ln1_b", D_MODEL), ("ln2_g", D_MODEL), ("ln2_b", D_MODEL),
)
_VEC_LAYOUT = {}
_VEC_LEN = 0
for _name, _n in _VEC_FIELDS:
    assert _n % 128 == 0
    _VEC_LAYOUT[_name] = (_VEC_LEN, _n)
    _VEC_LEN += _n


def _vec_views(vec_ref):
    return {name: vec_ref.at[:, off:off + n] for name, (off, n) in _VEC_LAYOUT.items()}


def _dot(a, b):
    return jnp.dot(a, b, preferred_element_type=F32)


def _dot_nt(a, b):
    return lax.dot_general(a, b, (((1,), (1,)), ((), ())), preferred_element_type=F32)


def _layer_norm(x, g, b, eps=1e-5):
    mu = jnp.mean(x, axis=-1, keepdims=True)
    xc = x - mu
    var = jnp.mean(xc * xc, axis=-1, keepdims=True)
    return xc * lax.rsqrt(var + eps) * g + b


def _rms_norm(x, g, eps=1e-6):
    return x * lax.rsqrt(jnp.mean(x * x, axis=-1, keepdims=True) + eps) * g


def _gelu_tanh(x):
    c = 0.7978845608028654
    return 0.5 * x * (1.0 + jnp.tanh(c * (x + 0.044715 * (x * x * x))))


def _softplus(x):
    return jnp.maximum(x, 0.0) + jnp.log1p(jnp.exp(-jnp.abs(x)))


def _swap_halves(x):
    w = x.shape[-1]
    lane = lax.broadcasted_iota(jnp.int32, (1, w), 1)
    first_half = (lane % HEAD_DIM) < HALF_DIM
    return jnp.where(first_half, pltpu.roll(x, w - HALF_DIM, 1), pltpu.roll(x, HALF_DIM, 1))


def _rope(x, cos, sin_signed):
    reps = x.shape[-1] // LANES
    if reps > 1:
        cos = jnp.concatenate([cos] * reps, axis=1)
        sin_signed = jnp.concatenate([sin_signed] * reps, axis=1)
    return x * cos + _swap_halves(x) * sin_signed


def _gate_matmuls(xc, wa_ref, wx_ref):
    xcb = xc.astype(BF16)
    half = LRU_WIDTH // 2
    lo, hi = xcb[:, :half], xcb[:, half:]
    ra = jnp.concatenate([_dot(lo, wa_ref[0]), _dot(hi, wa_ref[1])], axis=1)
    ia = jnp.concatenate([_dot(lo, wx_ref[0]), _dot(hi, wx_ref[1])], axis=1)
    return ra, ia


def _gate_nonlin(xc, ra_half, ia_half, b_a, b_x, lam):
    c_half = (-0.5 * LRU_C) * _softplus(-lam)
    t_r = jnp.tanh(ra_half + 0.5 * b_a)
    neg_log_a = (-c_half) * t_r - c_half
    a = jnp.exp2(neg_log_a * (-LOG2E))
    y = jnp.tanh(neg_log_a) * (a * a + 1.0)
    mult = jnp.where(y > 0.0, y * lax.rsqrt(y), 0.0)
    t_i = jnp.tanh(ia_half + 0.5 * b_x)
    hx = 0.5 * xc
    u = mult * (hx * t_i + hx)
    return a, u


def _lane_halves(x, kvh):
    lane = lax.broadcasted_iota(jnp.int32, (1, x.shape[1]), 1)
    low = lane < HEAD_DIM
    rolled = pltpu.roll(x, HEAD_DIM, 1)
    if kvh == 0:
        return jnp.where(low, x, 0.0), jnp.where(low, 0.0, rolled)
    return jnp.where(low, rolled, 0.0), jnp.where(low, 0.0, x)


def _stacked_halves(x, kvh):
    return jnp.concatenate(_lane_halves(x, kvh), axis=0).astype(BF16)


def _sink_softmax_pair(scores, mask, sink_lo, sink_hi):
    s_len = mask.shape[1]
    ps, invs = [], []
    for half, sink in ((0, sink_lo), (1, sink_hi)):
        sh = jnp.where(mask, scores[:, half * s_len:(half + 1) * s_len], -jnp.inf)
        m = jnp.maximum(jnp.max(sh, axis=-1, keepdims=True), sink)
        p = jnp.exp2(sh - m)
        denom = jnp.sum(p, axis=-1, keepdims=True) + jnp.exp2(sink - m)
        ps.append(p.astype(BF16))
        invs.append(1.0 / denom)
    return jnp.concatenate(ps, axis=1), invs


def _pv_pair(probs, vv, invs):
    out = _dot(probs, vv)
    lane = lax.broadcasted_iota(jnp.int32, (1, LANES), 1)
    return out * jnp.where(lane < HEAD_DIM, invs[0], invs[1])


def _scan_groups(a, u, h0):
    rows, width = a.shape
    groups = rows // SUBLANES
    a3 = a.reshape(groups, SUBLANES, width)
    u3 = u.reshape(groups, SUBLANES, width)
    row = lax.broadcasted_iota(jnp.int32, (1, SUBLANES, 1), 1)
    d = 1
    while d < SUBLANES:
        valid = row >= d
        u3 = u3 + a3 * jnp.where(valid, pltpu.roll(u3, d, 1), 0.0)
        a3 = a3 * jnp.where(valid, pltpu.roll(a3, d, 1), 1.0)
        d *= 2
    h = h0
    hs = []
    for g in range(groups):
        hg = u3[g] + a3[g] * h
        h = hg[SUBLANES - 1:SUBLANES, :]
        hs.append(hg)
    return jnp.concatenate(hs, axis=0), h


class _FfnStream:
    def __init__(self, lhs, acc, wg_ref, wu_ref, wd_ref, first, last):
        self.lhs, self.acc = lhs, acc
        self.wg_ref, self.wu_ref, self.wd_ref = wg_ref, wu_ref, wd_ref
        self.next, self.last = first, last
        self.pending = []

    def _drain(self):
        gt, up, c = self.pending.pop(0)
        hid = (gt * jax.nn.sigmoid(gt) * up).astype(BF16)
        part = _dot(hid, self.wd_ref[c * FFN_CHUNK:(c + 1) * FFN_CHUNK, :])
        self.acc = part if self.acc is None else self.acc + part

    def emit(self, count):
        for _ in range(count):
            if self.next < self.last:
                cols = slice(self.next * FFN_CHUNK, (self.next + 1) * FFN_CHUNK)
                self.pending.append((_dot(self.lhs, self.wg_ref[:, cols]), _dot(self.lhs, self.wu_ref[:, cols]),
                                     self.next))
                self.next += 1
                if len(self.pending) > 1:
                    self._drain()

    def finish(self):
        self.emit(self.last - self.next)
        while self.pending:
            self._drain()
        return self.acc


def _prompt_layer_kernel(x_ref, rope_off_ref, rope_base_ref, vec_ref, sinks_ref,
                         w_in_ref, wa_ref, wx_ref, w_out_ref, wg_ref, wu_ref, wd_ref,
                         y_ref, conv_out_ref, lru_out_ref, k_out_ref, v_out_ref,
                         xr_ext, h_carry, k_prev, v_prev, h1_scr, hb_scr, acc_scr, *, n_t, n_tiles):
    vec = _vec_views(vec_ref)
    b_in_ref, conv_b_ref, b_a_ref, b_x_ref, lam_ref = (vec[n] for n in ("b_in", "conv_b", "b_a", "b_x", "lam"))
    g_lru_ref, g_attn_ref, b_out_ref = (vec[n] for n in ("g_lru", "g_attn", "b_out"))
    ln1_g_ref, ln1_b_ref, ln2_g_ref, ln2_b_ref = (vec[n] for n in ("ln1_g", "ln1_b", "ln2_g", "ln2_b"))
    conv_w_rows = [vec["conv_w%d" % j] for j in range(CONV_WIDTH)]
    g_idx = pl.program_id(0)
    t_idx = lax.rem(g_idx, n_t)
    tile = x_ref.shape[0]
    hist = SUBLANES

    @pl.when(g_idx == 0)
    def _():
        h1_scr[...] = jnp.zeros_like(h1_scr)
        hb_scr[...] = jnp.zeros_like(hb_scr)
        acc_scr[...] = jnp.zeros_like(acc_scr)

    @pl.when(t_idx == 0)
    def _():
        xr_ext[0:hist, :] = jnp.zeros((hist, LRU_WIDTH), F32)
        h_carry[...] = jnp.zeros_like(h_carry)
        k_prev[...] = jnp.zeros_like(k_prev)
        v_prev[...] = jnp.zeros_like(v_prev)

    n_chunks = D_FF // FFN_CHUNK
    prev_ffn = _FfnStream(hb_scr[...], acc_scr[...], wg_ref, wu_ref, wd_ref, FFN_HEAD_CHUNKS, n_chunks)
    ffn_chunks = prev_ffn.emit

    x = x_ref[...]
    xb = x.astype(BF16)

    def proj(c0, width):
        return _dot(xb, w_in_ref[:, c0:c0 + width]) + b_in_ref[:, c0:c0 + width]

    xr = proj(_XR0, LRU_WIDTH)
    xr_ext[hist:hist + tile, :] = xr
    q_raw = proj(_Q0, Q_DIM)
    k_raw = proj(_K0, KV_DIM)
    v = proj(_V0, KV_DIM)
    gate = proj(_GATE0, LRU_WIDTH)

    ffn_chunks(1)
    xc = conv_b_ref[...] + xr * conv_w_rows[CONV_WIDTH - 1][...]
    for j in range(CONV_WIDTH - 1):
        shift = CONV_WIDTH - 1 - j
        xc = xc + xr_ext[hist - shift:hist - shift + tile, :] * conv_w_rows[j][...]
    xr_ext[0:hist, :] = xr_ext[tile:tile + hist, :]

    ra_half, ia_half = _gate_matmuls(xc, wa_ref, wx_ref)

    ffn_chunks(1)
    cos_o, sin_o = rope_off_ref[0], rope_off_ref[1]
    base = rope_base_ref[pl.ds(t_idx, 1)][0]
    cos = base[0:1] * cos_o - base[1:2] * sin_o
    sin = base[3:4] * cos_o + base[2:3] * sin_o
    qb = _rope(q_raw, cos * Q_SCALE, sin * Q_SCALE).astype(BF16)
    k = _rope(k_raw, cos, sin)

    qi = lax.broadcasted_iota(jnp.int32, (BLOCK_Q, 2 * BLOCK_Q), 0)
    kj = lax.broadcasted_iota(jnp.int32, (BLOCK_Q, 2 * BLOCK_Q), 1)
    band = (kj >= qi) & (kj <= qi + WINDOW)
    first_lo = jnp.where(t_idx == 0, BLOCK_Q, 0)
    band_first = band & (kj >= first_lo)

    n_blocks = tile // BLOCK_Q
    units = []
    for n in range(n_blocks):
        rows = slice(n * BLOCK_Q, (n + 1) * BLOCK_Q)
        if n == 0:
            k2 = jnp.concatenate([k_prev[...], k[rows]], axis=0)
            v2 = jnp.concatenate([v_prev[...], v[rows]], axis=0)
            mask = band_first
        else:
            k2 = k[(n - 1) * BLOCK_Q:(n + 1) * BLOCK_Q]
            v2 = v[(n - 1) * BLOCK_Q:(n + 1) * BLOCK_Q]
            mask = band
        for kvh in range(N_KV_HEADS):
            kk = _stacked_halves(k2, kvh)
            vv = _stacked_halves(v2, kvh)
            for cc in range(2):
                c = 2 * kvh + cc
                units.append((_dot_nt(qb[rows, c * LANES:(c + 1) * LANES], kk), vv, mask, c))
    k_prev[...] = k[tile - BLOCK_Q:tile]
    v_prev[...] = v[tile - BLOCK_Q:tile]

    ffn_chunks(1)
    a, u = _gate_nonlin(xc, ra_half, ia_half, b_a_ref[...], b_x_ref[...], lam_ref[...])

    ffn_chunks(1)
    soft = [_sink_softmax_pair(s, mask, sinks_ref[2 * c] * LOG2E, sinks_ref[2 * c + 1] * LOG2E)
            for s, _, mask, c in units]

    outs = [_pv_pair(p, vv, invs) for (p, invs), (_, vv, _, _) in zip(soft, units)]

    ffn_chunks(2)
    h_all, h_last = _scan_groups(a, u, h_carry[...])
    h_carry[...] = h_last

    y_lru = _rms_norm(_gelu_tanh(gate) * h_all, g_lru_ref[...])
    per_block = N_HEADS // 2
    y_attn = jnp.concatenate(
        [jnp.concatenate(outs[n * per_block:(n + 1) * per_block], axis=1) for n in range(n_blocks)], axis=0)
    y_attn = _rms_norm(y_attn, g_attn_ref[...])

    y_mix = (_dot(y_lru.astype(BF16), w_out_ref[0:LRU_WIDTH, :])
             + _dot(y_attn.astype(BF16), w_out_ref[LRU_WIDTH:LRU_WIDTH + Q_DIM, :]) + b_out_ref[...])
    h1 = _layer_norm(ALPHA * x + y_mix, ln1_g_ref[...], ln1_b_ref[...])
    acc_prev = prev_ffn.finish()
    h_prev = h1_scr[...]
    hb = h1.astype(BF16)
    h1_scr[...] = h1
    hb_scr[...] = hb

    head_ffn = _FfnStream(hb, None, wg_ref, wu_ref, wd_ref, 0, FFN_HEAD_CHUNKS)
    head_ffn.emit(1)
    y_ref[...] = _layer_norm(ALPHA * h_prev + acc_prev, ln2_g_ref[...], ln2_b_ref[...])
    acc_scr[...] = head_ffn.finish()

    @pl.when((t_idx == n_t - 1) & (g_idx < n_tiles))
    def _():
        conv_out_ref[...] = xr_ext[hist - (CONV_WIDTH - 1):hist, :]
        lru_out_ref[...] = h_last
        k_out_ref[...] = k[tile - WINDOW:tile]
        v_out_ref[...] = v[tile - WINDOW:tile]


def _full_spec(shape):
    zeros = (0,) * len(shape)
    return pl.BlockSpec(shape, lambda *_: zeros, pipeline_mode=pl.Buffered(1))


def _vec_spec():
    return _full_spec((1, _VEC_LEN))


def _mixer_weight_specs():
    return [
        _vec_spec(), pl.BlockSpec(memory_space=pltpu.SMEM),
        _full_spec((D_MODEL, IN_DIM)),
        _full_spec((2, LRU_WIDTH // 2, LRU_WIDTH // 2)), _full_spec((2, LRU_WIDTH // 2, LRU_WIDTH // 2)),
        _full_spec((LRU_WIDTH + Q_DIM, D_MODEL)),
    ]


def _ffn_weight_specs():
    return [_full_spec((D_MODEL, D_FF)), _full_spec((D_MODEL, D_FF)), _full_spec((D_FF, D_MODEL))]


def _prompt_layer(x, mixer_w, ffn_w):
    bsz, seq, _ = x.shape
    tile = PROMPT_TILE
    assert seq % tile == 0 and tile % BLOCK_Q == 0 and tile >= WINDOW and D_FF % FFN_CHUNK == 0
    n_t = seq // tile
    n_tiles = bsz * n_t
    keep = min(WINDOW, seq)

    def cur(g):
        gi = jnp.minimum(g, n_tiles - 1)
        return lax.div(gi, n_t), lax.rem(gi, n_t)

    def prev(g):
        go = jnp.maximum(g - 1, 0)
        return lax.div(go, n_t), lax.rem(go, n_t)

    in_specs = [pl.BlockSpec((None, tile, D_MODEL), lambda g: (*cur(g), 0)),
                _full_spec((2, tile, LANES)), _full_spec((n_t, 4, LANES))]
    in_specs += _mixer_weight_specs() + _ffn_weight_specs()
    out_shape = (
        jax.ShapeDtypeStruct((bsz, seq, D_MODEL), F32),
        jax.ShapeDtypeStruct((bsz, CONV_WIDTH - 1, LRU_WIDTH), F32),
        jax.ShapeDtypeStruct((bsz, 1, LRU_WIDTH), F32),
        jax.ShapeDtypeStruct((bsz, keep, KV_DIM), F32),
        jax.ShapeDtypeStruct((bsz, keep, KV_DIM), F32),
    )
    out_specs = (
        pl.BlockSpec((None, tile, D_MODEL), lambda g: (*prev(g), 0)),
        pl.BlockSpec((None, CONV_WIDTH - 1, LRU_WIDTH), lambda g: (cur(g)[0], 0, 0)),
        pl.BlockSpec((None, 1, LRU_WIDTH), lambda g: (cur(g)[0], 0, 0)),
        pl.BlockSpec((None, keep, KV_DIM), lambda g: (cur(g)[0], 0, 0)),
        pl.BlockSpec((None, keep, KV_DIM), lambda g: (cur(g)[0], 0, 0)),
    )
    scratch = [
        pltpu.VMEM((tile + SUBLANES, LRU_WIDTH), F32),
        pltpu.VMEM((1, LRU_WIDTH), F32),
        pltpu.VMEM((BLOCK_Q, KV_DIM), F32),
        pltpu.VMEM((BLOCK_Q, KV_DIM), F32),
        pltpu.VMEM((tile, D_MODEL), F32),
        pltpu.VMEM((tile, D_MODEL), BF16),
        pltpu.VMEM((tile, D_MODEL), F32),
    ]
    return pl.pallas_call(
        functools.partial(_prompt_layer_kernel, n_t=n_t, n_tiles=n_tiles),
        grid=(n_tiles + 1,),
        in_specs=in_specs,
        out_specs=out_specs,
        out_shape=out_shape,
        scratch_shapes=scratch,
        compiler_params=pltpu.CompilerParams(
            dimension_semantics=("arbitrary",), vmem_limit_bytes=VMEM_LIMIT_BYTES),
        name="prompt_layer",
    )(x, *_rope_tile_tables(tile, n_t), *mixer_w, *ffn_w)


def _ffn_kernel(h_ref, vec_ref, wg_ref, wu_ref, wd_ref, o_ref):
    vec = _vec_views(vec_ref)
    ln_g_ref, ln_b_ref = vec["ln2_g"], vec["ln2_b"]
    slabs = h_ref.shape[1] // D_MODEL
    h = jnp.concatenate([h_ref[:, s * D_MODEL:(s + 1) * D_MODEL] for s in range(slabs)], axis=0)
    acc = _FfnStream(h.astype(BF16), None, wg_ref, wu_ref, wd_ref, 0, D_FF // FFN_CHUNK).finish()
    out = _layer_norm(ALPHA * h + acc, ln_g_ref[...], ln_b_ref[...])
    rows = h_ref.shape[0]
    for s in range(slabs):
        o_ref[:, s * D_MODEL:(s + 1) * D_MODEL] = out[s * rows:(s + 1) * rows]


def _ffn(h, vec, wg, wu, wd, tile):
    rows, width = h.shape
    assert rows % tile == 0 and width % D_MODEL == 0 and D_FF % FFN_CHUNK == 0
    return pl.pallas_call(
        _ffn_kernel,
        grid=(rows // tile,),
        in_specs=[pl.BlockSpec((tile, width), lambda i: (i, 0)), _vec_spec()] + _ffn_weight_specs(),
        out_specs=pl.BlockSpec((tile, width), lambda i: (i, 0)),
        out_shape=jax.ShapeDtypeStruct((rows, width), F32),
        compiler_params=pltpu.CompilerParams(
            dimension_semantics=("arbitrary",), vmem_limit_bytes=VMEM_LIMIT_BYTES),
        name="ffn",
    )(h, vec, wg, wu, wd)


def _decode_mixer_kernel(x_ref, kc_ref, vc_ref, sconv_ref, slru_ref, cos_ref, sin_ref, vec_ref, sinks_ref,
                         w_in_ref, wa_ref, wx_ref, w_out_ref,
                         h1_ref, conv_out_ref, lru_out_ref, k_out_ref, v_out_ref):
    vec = _vec_views(vec_ref)
    b_in_ref, conv_b_ref, b_a_ref, b_x_ref, lam_ref = (vec[n] for n in ("b_in", "conv_b", "b_a", "b_x", "lam"))
    g_lru_ref, g_attn_ref, b_out_ref, ln_g_ref, ln_b_ref = (
        vec[n] for n in ("g_lru", "g_attn", "b_out", "ln1_g", "ln1_b"))
    conv_w_rows = [vec["conv_w%d" % j] for j in range(CONV_WIDTH)]
    nb = x_ref.shape[0]
    steps = x_ref.shape[1] // D_MODEL
    win = kc_ref.shape[1]

    x = jnp.concatenate([x_ref[:, t * D_MODEL:(t + 1) * D_MODEL] for t in range(steps)], axis=0)
    xb = x.astype(BF16)

    def proj(c0, width):
        return _dot(xb, w_in_ref[:, c0:c0 + width]) + b_in_ref[:, c0:c0 + width]

    def step_rows(arr, t):
        return arr[t * nb:(t + 1) * nb]

    xr = proj(_XR0, LRU_WIDTH)
    ext = [sconv_ref[j] for j in range(CONV_WIDTH - 1)]
    ext += [step_rows(xr, t) for t in range(steps)]
    xc_steps = []
    for t in range(steps):
        acc = conv_b_ref[...]
        for j in range(CONV_WIDTH):
            acc = acc + ext[t + j] * conv_w_rows[j][...]
        xc_steps.append(acc)
    for j in range(CONV_WIDTH - 1):
        conv_out_ref[j] = ext[steps + j]
    xc = jnp.concatenate(xc_steps, axis=0)

    ra_half, ia_half = _gate_matmuls(xc, wa_ref, wx_ref)
    a, u = _gate_nonlin(xc, ra_half, ia_half, b_a_ref[...], b_x_ref[...], lam_ref[...])
    h = slru_ref[...]
    hs = []
    for t in range(steps):
        h = step_rows(a, t) * h + step_rows(u, t)
        hs.append(h)
    lru_out_ref[...] = h
    gate = proj(_GATE0, LRU_WIDTH)
    y_lru = _rms_norm(_gelu_tanh(gate) * jnp.concatenate(hs, axis=0), g_lru_ref[...])

    def table(ref):
        return jnp.concatenate([jnp.broadcast_to(ref[t:t + 1, :], (nb, LANES)) for t in range(steps)], axis=0)

    cos, sin = table(cos_ref), table(sin_ref)
    q = _rope(proj(_Q0, Q_DIM), cos * Q_SCALE, sin * Q_SCALE)
    k = _rope(proj(_K0, KV_DIM), cos, sin)
    v = proj(_V0, KV_DIM)

    k_out_ref[:, 0:win - steps, :] = kc_ref[:, steps:win, :]
    v_out_ref[:, 0:win - steps, :] = vc_ref[:, steps:win, :]
    for t in range(steps):
        k_out_ref[:, win - steps + t, :] = step_rows(k, t)
        v_out_ref[:, win - steps + t, :] = step_rows(v, t)

    sub = DEC_SUB
    n_sub = nb // sub
    n_cache = sub * win
    n_keys = n_cache + steps * sub
    qr = lax.broadcasted_iota(jnp.int32, (steps * sub, n_keys), 0)
    kc_col = lax.broadcasted_iota(jnp.int32, (steps * sub, n_keys), 1)
    q_step, q_seq = qr // sub, qr % sub
    cached = kc_col < n_cache
    new_col = kc_col - n_cache
    key_seq = jnp.where(cached, kc_col // win, new_col % sub)
    key_lo = jnp.where(cached, q_step + (win - WINDOW), 0)
    key_hi = jnp.where(cached, win - 1, q_step)
    key_idx = jnp.where(cached, kc_col % win, new_col // sub)
    mask = (key_seq == q_seq) & (key_idx >= key_lo) & (key_idx <= key_hi)

    def sub_rows(arr, j):
        return jnp.concatenate([arr[t * nb + j * sub:t * nb + (j + 1) * sub] for t in range(steps)], axis=0)

    units = []
    for j in range(n_sub):
        qj = sub_rows(q, j).astype(BF16)
        kall = jnp.concatenate([kc_ref[j * sub:(j + 1) * sub].reshape(n_cache, KV_DIM), sub_rows(k, j)], axis=0)
        vall = jnp.concatenate([vc_ref[j * sub:(j + 1) * sub].reshape(n_cache, KV_DIM), sub_rows(v, j)], axis=0)
        for kvh in range(N_KV_HEADS):
            kk = _stacked_halves(kall, kvh)
            vv = _stacked_halves(vall, kvh)
            for cc in range(2):
                c = 2 * kvh + cc
                units.append((_dot_nt(qj[:, c * LANES:(c + 1) * LANES], kk), vv, c))
    soft = [_sink_softmax_pair(s, mask, sinks_ref[2 * c] * LOG2E, sinks_ref[2 * c + 1] * LOG2E) for s, _, c in units]
    outs = [_pv_pair(p, vv, invs) for (p, invs), (_, vv, _) in zip(soft, units)]
    per_sub = N_HEADS // 2
    attn_sub = [jnp.concatenate(outs[j * per_sub:(j + 1) * per_sub], axis=1) for j in range(n_sub)]
    y_attn = jnp.concatenate(
        [attn_sub[j][t * sub:(t + 1) * sub] for t in range(steps) for j in range(n_sub)], axis=0)
    y_attn = _rms_norm(y_attn, g_attn_ref[...])

    y_mix = (_dot(y_lru.astype(BF16), w_out_ref[0:LRU_WIDTH, :])
             + _dot(y_attn.astype(BF16), w_out_ref[LRU_WIDTH:LRU_WIDTH + Q_DIM, :]) + b_out_ref[...])
    h1 = _layer_norm(ALPHA * x + y_mix, ln_g_ref[...], ln_b_ref[...])
    for t in range(steps):
        h1_ref[:, t * D_MODEL:(t + 1) * D_MODEL] = step_rows(h1, t)


def _decode_mixer(x2d, kc, vc, sconv, slru, tables, weights):
    nseq, width = x2d.shape
    steps = width // D_MODEL
    win = kc.shape[1]
    nb = DEC_BLOCK
    assert nseq % nb == 0 and nb % DEC_SUB == 0 and DEC_SUB % SUBLANES == 0 and steps <= win
    conv_spec = pl.BlockSpec((CONV_WIDTH - 1, nb, LRU_WIDTH), lambda i: (0, i, 0))
    in_specs = [
        pl.BlockSpec((nb, width), lambda i: (i, 0)),
        pl.BlockSpec((nb, win, KV_DIM), lambda i: (i, 0, 0)),
        pl.BlockSpec((nb, win, KV_DIM), lambda i: (i, 0, 0)),
        conv_spec,
        pl.BlockSpec((nb, LRU_WIDTH), lambda i: (i, 0)),
    ] + [_full_spec((steps, LANES))] * 2 + _mixer_weight_specs()
    out_shape = (
        jax.ShapeDtypeStruct((nseq, width), F32),
        jax.ShapeDtypeStruct((CONV_WIDTH - 1, nseq, LRU_WIDTH), F32),
        jax.ShapeDtypeStruct((nseq, LRU_WIDTH), F32),
        jax.ShapeDtypeStruct((nseq, win, KV_DIM), F32),
        jax.ShapeDtypeStruct((nseq, win, KV_DIM), F32),
    )
    out_specs = (
        pl.BlockSpec((nb, width), lambda i: (i, 0)),
        conv_spec,
        pl.BlockSpec((nb, LRU_WIDTH), lambda i: (i, 0)),
        pl.BlockSpec((nb, win, KV_DIM), lambda i: (i, 0, 0)),
        pl.BlockSpec((nb, win, KV_DIM), lambda i: (i, 0, 0)),
    )
    return pl.pallas_call(
        _decode_mixer_kernel,
        grid=(nseq // nb,),
        in_specs=in_specs,
        out_specs=out_specs,
        out_shape=out_shape,
        compiler_params=pltpu.CompilerParams(
            dimension_semantics=("arbitrary",), vmem_limit_bytes=VMEM_LIMIT_BYTES),
        name="decode_mixer",
    )(x2d, kc, vc, sconv, slru, *tables, *weights)


def _rope_lanes():
    lane = jnp.arange(LANES, dtype=jnp.int32)
    inv = ROPE_THETA ** (-(lane % HALF_DIM).astype(F32) / HALF_DIM)
    sign = jnp.where(lane % HEAD_DIM < HALF_DIM, -1.0, 1.0).astype(F32)
    return inv, sign


def _rope_tables(positions):
    inv, sign = _rope_lanes()
    ang = positions.astype(F32)[:, None] * inv[None, :]
    return jnp.cos(ang), sign[None, :] * jnp.sin(ang)


def _rope_tile_tables(tile, n_t):
    inv, sign = _rope_lanes()
    off = jnp.arange(tile, dtype=F32)[:, None] * inv[None, :]
    base = (jnp.arange(n_t, dtype=F32) * tile)[:, None] * inv[None, :]
    cos_b, sin_b = jnp.cos(base), jnp.sin(base)
    return (jnp.stack([jnp.cos(off), jnp.sin(off)]),
            jnp.stack([cos_b, sin_b, sign[None, :] * cos_b, sign[None, :] * sin_b], axis=1))


def _block_diag_halves(w):
    per = LRU_BLOCKS // 2
    d = w.shape[-1]
    w4 = w.reshape(2, per, d, d)
    eye = jnp.eye(per, dtype=w.dtype)
    return jnp.einsum('paij,ab->paibj', w4, eye).reshape(2, per * d, per * d)


def kernel(x_prompt, x_sample, cache_k_win, cache_v_win, state_conv, state_lru, w_in, b_in, conv_w, conv_b,
           w_a, b_a, w_x, b_x, lru_lambda, sinks, g_lru, g_attn, w_out, b_out, ln1_g, ln1_b,
           w_gate, w_up, w_down, ln2_g, ln2_b):
    depth = w_in.shape[0]
    assert depth == 1
    l = 0
    bsz, seq, _ = x_prompt.shape
    nseq, steps, _ = x_sample.shape
    win = cache_k_win.shape[2]

    fields = dict(b_in=b_in, conv_b=conv_b, b_a=b_a, b_x=b_x, lam=lru_lambda, g_lru=g_lru, g_attn=g_attn,
                  b_out=b_out, ln1_g=ln1_g, ln1_b=ln1_b, ln2_g=ln2_g, ln2_b=ln2_b)
    fields.update({"conv_w%d" % j: conv_w[:, j] for j in range(CONV_WIDTH)})
    vec = jnp.concatenate([fields[name][l].reshape(-1) for name, _ in _VEC_FIELDS]).reshape(1, _VEC_LEN)

    mixer_w = (
        vec, sinks[l], w_in[l].astype(BF16),
        (0.5 * _block_diag_halves(w_a[l])).astype(BF16), (0.5 * _block_diag_halves(w_x[l])).astype(BF16),
        w_out[l].astype(BF16),
    )
    ffn_w = (w_gate[l].astype(BF16), w_up[l].astype(BF16), w_down[l].astype(BF16))

    y_p, conv_p, lru_p, k_p, v_p = _prompt_layer(x_prompt, mixer_w, ffn_w)

    h1_s, conv_s, lru_s, k_s, v_s = _decode_mixer(
        x_sample.reshape(nseq, steps * D_MODEL),
        cache_k_win[l].reshape(nseq, win, KV_DIM), cache_v_win[l].reshape(nseq, win, KV_DIM),
        jnp.transpose(state_conv[l], (1, 0, 2)), state_lru[l],
        _rope_tables(PAST_LEN + jnp.arange(steps, dtype=jnp.int32)), mixer_w)
    y_s = _ffn(h1_s, vec, *ffn_w, tile=nseq).reshape(nseq, steps, D_MODEL)

    keep = k_p.shape[1]
    return (
        y_p, y_s,
        conv_p[None], lru_p.reshape(1, bsz, LRU_WIDTH),
        k_p.reshape(1, bsz, keep, N_KV_HEADS, HEAD_DIM), v_p.reshape(1, bsz, keep, N_KV_HEADS, HEAD_DIM),
        jnp.transpose(conv_s, (1, 0, 2))[None], lru_s[None],
        k_s.reshape(1, nseq, win, N_KV_HEADS, HEAD_DIM), v_s.reshape(1, nseq, win, N_KV_HEADS, HEAD_DIM),
    )
```

```python
import functools

import jax
import jax.numpy as jnp
from jax import lax
from jax.experimental import pallas as pl
from jax.experimental.pallas import tpu as pltpu

D_MODEL = 1024
LRU_WIDTH = 512
LRU_BLOCKS = 8
CONV_WIDTH = 4
LRU_C = 8.0
N_HEADS = 8
N_KV_HEADS = 2
HEAD_DIM = 64
HALF_DIM = HEAD_DIM // 2
Q_DIM = N_HEADS * HEAD_DIM
KV_DIM = N_KV_HEADS * HEAD_DIM
IN_DIM = 2 * LRU_WIDTH + Q_DIM + 2 * KV_DIM
WINDOW = 128
BLOCK_Q = 128
ROPE_THETA = 10000.0
ATTN_SCALE = HEAD_DIM ** -0.5
D_FF = 2816
ALPHA = 2.0 ** 0.25
PAST_LEN = 16384
LOG2E = 1.4426950408889634
Q_SCALE = ATTN_SCALE * LOG2E

LANES = 128
SUBLANES = 8
VMEM_LIMIT_BYTES = 56 * 1024 * 1024

_XR0, _GATE0, _Q0, _K0, _V0 = 0, LRU_WIDTH, 2 * LRU_WIDTH, 2 * LRU_WIDTH + Q_DIM, 2 * LRU_WIDTH + Q_DIM + KV_DIM

PROMPT_TILE = 512
FFN_CHUNK = 256
FFN_HEAD_CHUNKS = 3
DEC_BLOCK = 32
DEC_SUB = 8

BF16 = jnp.bfloat16
F32 = jnp.float32

_VEC_FIELDS = (
    ("b_in", IN_DIM), ("conv_w0", LRU_WIDTH), ("conv_w1", LRU_WIDTH), ("conv_w2", LRU_WIDTH), ("conv_w3", LRU_WIDTH),
    ("conv_b", LRU_WIDTH), ("b_a", LRU_WIDTH), ("b_x", LRU_WIDTH), ("lam", LRU_WIDTH),
    ("g_lru", LRU_WIDTH), ("g_attn", Q_DIM), ("b_out", D_MODEL),
    ("ln1_g", D_MODEL), ("ln1_b", D_MODEL), ("ln2_g", D_MODEL), ("ln2_b", D_MODEL),
)
_VEC_LAYOUT = {}
_VEC_LEN = 0
for _name, _n in _VEC_FIELDS:
    assert _n % 128 == 0
    _VEC_LAYOUT[_name] = (_VEC_LEN, _n)
    _VEC_LEN += _n


def _vec_views(vec_ref):
    return {name: vec_ref.at[:, off:off + n] for name, (off, n) in _VEC_LAYOUT.items()}


def _dot(a, b):
    return jnp.dot(a, b, preferred_element_type=F32)


def _dot_nt(a, b):
    return lax.dot_general(a, b, (((1,), (1,)), ((), ())), preferred_element_type=F32)


def _layer_norm(x, g, b, eps=1e-5):
    mu = jnp.mean(x, axis=-1, keepdims=True)
    xc = x - mu
    var = jnp.mean(xc * xc, axis=-1, keepdims=True)
    return xc * lax.rsqrt(var + eps) * g + b


def _rms_norm(x, g, eps=1e-6):
    return x * lax.rsqrt(jnp.mean(x * x, axis=-1, keepdims=True) + eps) * g


def _gelu_tanh(x):
    c = 0.7978845608028654
    return 0.5 * x * (1.0 + jnp.tanh(c * (x + 0.044715 * (x * x * x))))


def _softplus(x):
    return jnp.maximum(x, 0.0) + jnp.log1p(jnp.exp(-jnp.abs(x)))


def _swap_halves(x):
    w = x.shape[-1]
    lane = lax.broadcasted_iota(jnp.int32, (1, w), 1)
    first_half = (lane % HEAD_DIM) < HALF_DIM
    return jnp.where(first_half, pltpu.roll(x, w - HALF_DIM, 1), pltpu.roll(x, HALF_DIM, 1))


def _rope(x, cos, sin_signed):
    reps = x.shape[-1] // LANES
    if reps > 1:
        cos = jnp.concatenate([cos] * reps, axis=1)
        sin_signed = jnp.concatenate([sin_signed] * reps, axis=1)
    return x * cos + _swap_halves(x) * sin_signed


def _gate_matmuls(xc, wa_ref, wx_ref):
    xcb = xc.astype(BF16)
    half = LRU_WIDTH // 2
    lo, hi = xcb[:, :half], xcb[:, half:]
    ra = jnp.concatenate([_dot(lo, wa_ref[0]), _dot(hi, wa_ref[1])], axis=1)
    ia = jnp.concatenate([_dot(lo, wx_ref[0]), _dot(hi, wx_ref[1])], axis=1)
    return ra, ia


def _gate_nonlin(xc, ra_half, ia_half, b_a, b_x, lam):
    c_half = (-0.5 * LRU_C) * _softplus(-lam)
    t_r = jnp.tanh(ra_half + 0.5 * b_a)
    neg_log_a = (-c_half) * t_r - c_half
    a = jnp.exp2(neg_log_a * (-LOG2E))
    y = jnp.tanh(neg_log_a) * (a * a + 1.0)
    mult = jnp.where(y > 0.0, y * lax.rsqrt(y), 0.0)
    t_i = jnp.tanh(ia_half + 0.5 * b_x)
    hx = 0.5 * xc
    u = mult * (hx * t_i + hx)
    return a, u


def _lane_halves(x, kvh):
    lane = lax.broadcasted_iota(jnp.int32, (1, x.shape[1]), 1)
    low = lane < HEAD_DIM
    rolled = pltpu.roll(x, HEAD_DIM, 1)
    if kvh == 0:
        return jnp.where(low, x, 0.0), jnp.where(low, 0.0, rolled)
    return jnp.where(low, rolled, 0.0), jnp.where(low, 0.0, x)


def _stacked_halves(x, kvh):
    return jnp.concatenate(_lane_halves(x, kvh), axis=0).astype(BF16)


def _sink_softmax_pair(scores, mask, sink_lo, sink_hi):
    s_len = mask.shape[1]
    ps, invs = [], []
    for half, sink in ((0, sink_lo), (1, sink_hi)):
        sh = jnp.where(mask, scores[:, half * s_len:(half + 1) * s_len], -jnp.inf)
        m = jnp.maximum(jnp.max(sh, axis=-1, keepdims=True), sink)
        p = jnp.exp2(sh - m)
        denom = jnp.sum(p, axis=-1, keepdims=True) + jnp.exp2(sink - m)
        ps.append(p.astype(BF16))
        invs.append(1.0 / denom)
    return jnp.concatenate(ps, axis=1), invs


def _pv_pair(probs, vv, invs):
    out = _dot(probs, vv)
    lane = lax.broadcasted_iota(jnp.int32, (1, LANES), 1)
    return out * jnp.where(lane < HEAD_DIM, invs[0], invs[1])


def _scan_groups(a, u, h0):
    rows, width = a.shape
    groups = rows // SUBLANES
    a3 = a.reshape(groups, SUBLANES, width)
    u3 = u.reshape(groups, SUBLANES, width)
    row = lax.broadcasted_iota(jnp.int32, (1, SUBLANES, 1), 1)
    d = 1
    while d < SUBLANES:
        valid = row >= d
        u3 = u3 + a3 * jnp.where(valid, pltpu.roll(u3, d, 1), 0.0)
        a3 = a3 * jnp.where(valid, pltpu.roll(a3, d, 1), 1.0)
        d *= 2
    h = h0
    hs = []
    for g in range(groups):
        hg = u3[g] + a3[g] * h
        h = hg[SUBLANES - 1:SUBLANES, :]
        hs.append(hg)
    return jnp.concatenate(hs, axis=0), h


class _FfnStream:
    def __init__(self, lhs, acc, wg_ref, wu_ref, wd_ref, first, last):
        self.lhs, self.acc = lhs, acc
        self.wg_ref, self.wu_ref, self.wd_ref = wg_ref, wu_ref, wd_ref
        self.next, self.last = first, last
        self.pending = []

    def _drain(self):
        gt, up, c = self.pending.pop(0)
        hid = (gt * jax.nn.sigmoid(gt) * up).astype(BF16)
        part = _dot(hid, self.wd_ref[c * FFN_CHUNK:(c + 1) * FFN_CHUNK, :])
        self.acc = part if self.acc is None else self.acc + part

    def emit(self, count):
        for _ in range(count):
            if self.next < self.last:
                cols = slice(self.next * FFN_CHUNK, (self.next + 1) * FFN_CHUNK)
                self.pending.append((_dot(self.lhs, self.wg_ref[:, cols]), _dot(self.lhs, self.wu_ref[:, cols]),
                                     self.next))
                self.next += 1
                if len(self.pending) > 1:
                    self._drain()

    def finish(self):
        self.emit(self.last - self.next)
        while self.pending:
            self._drain()
        return self.acc


def _prompt_layer_kernel(x_ref, rope_off_ref, rope_base_ref, vec_ref, sinks_ref,
                         w_in_ref, wa_ref, wx_ref, w_out_ref, wg_ref, wu_ref, wd_ref,
                         y_ref, conv_out_ref, lru_out_ref, k_out_ref, v_out_ref,
                         xr_ext, h_carry, k_prev, v_prev, h1_scr, hb_scr, acc_scr, *, n_t, n_tiles):
    vec = _vec_views(vec_ref)
    b_in_ref, conv_b_ref, b_a_ref, b_x_ref, lam_ref = (vec[n] for n in ("b_in", "conv_b", "b_a", "b_x", "lam"))
    g_lru_ref, g_attn_ref, b_out_ref = (vec[n] for n in ("g_lru", "g_attn", "b_out"))
    ln1_g_ref, ln1_b_ref, ln2_g_ref, ln2_b_ref = (vec[n] for n in ("ln1_g", "ln1_b", "ln2_g", "ln2_b"))
    conv_w_rows = [vec["conv_w%d" % j] for j in range(CONV_WIDTH)]
    g_idx = pl.program_id(0)
    t_idx = lax.rem(g_idx, n_t)
    tile = x_ref.shape[0]
    hist = SUBLANES

    @pl.when(g_idx == 0)
    def _():
        h1_scr[...] = jnp.zeros_like(h1_scr)
        hb_scr[...] = jnp.zeros_like(hb_scr)
        acc_scr[...] = jnp.zeros_like(acc_scr)

    @pl.when(t_idx == 0)
    def _():
        xr_ext[0:hist, :] = jnp.zeros((hist, LRU_WIDTH), F32)
        h_carry[...] = jnp.zeros_like(h_carry)
        k_prev[...] = jnp.zeros_like(k_prev)
        v_prev[...] = jnp.zeros_like(v_prev)

    n_chunks = D_FF // FFN_CHUNK
    prev_ffn = _FfnStream(hb_scr[...], acc_scr[...], wg_ref, wu_ref, wd_ref, FFN_HEAD_CHUNKS, n_chunks)
    ffn_chunks = prev_ffn.emit

    x = x_ref[...]
    xb = x.astype(BF16)

    def proj(c0, width):
        return _dot(xb, w_in_ref[:, c0:c0 + width]) + b_in_ref[:, c0:c0 + width]

    xr = proj(_XR0, LRU_WIDTH)
    xr_ext[hist:hist + tile, :] = xr
    q_raw = proj(_Q0, Q_DIM)
    k_raw = proj(_K0, KV_DIM)
    v = proj(_V0, KV_DIM)
    gate = proj(_GATE0, LRU_WIDTH)

    ffn_chunks(1)
    xc = conv_b_ref[...] + xr * conv_w_rows[CONV_WIDTH - 1][...]
    for j in range(CONV_WIDTH - 1):
        shift = CONV_WIDTH - 1 - j
        xc = xc + xr_ext[hist - shift:hist - shift + tile, :] * conv_w_rows[j][...]
    xr_ext[0:hist, :] = xr_ext[tile:tile + hist, :]

    ra_half, ia_half = _gate_matmuls(xc, wa_ref, wx_ref)

    ffn_chunks(1)
    cos_o, sin_o = rope_off_ref[0], rope_off_ref[1]
    base = rope_base_ref[pl.ds(t_idx, 1)][0]
    cos = base[0:1] * cos_o - base[1:2] * sin_o
    sin = base[3:4] * cos_o + base[2:3] * sin_o
    qb = _rope(q_raw, cos * Q_SCALE, sin * Q_SCALE).astype(BF16)
    k = _rope(k_raw, cos, sin)

    qi = lax.broadcasted_iota(jnp.int32, (BLOCK_Q, 2 * BLOCK_Q), 0)
    kj = lax.broadcasted_iota(jnp.int32, (BLOCK_Q, 2 * BLOCK_Q), 1)
    band = (kj >= qi) & (kj <= qi + WINDOW)
    first_lo = jnp.where(t_idx == 0, BLOCK_Q, 0)
    band_first = band & (kj >= first_lo)

    n_blocks = tile // BLOCK_Q
    units = []
    for n in range(n_blocks):
        rows = slice(n * BLOCK_Q, (n + 1) * BLOCK_Q)
        if n == 0:
            k2 = jnp.concatenate([k_prev[...], k[rows]], axis=0)
            v2 = jnp.concatenate([v_prev[...], v[rows]], axis=0)
            mask = band_first
        else:
            k2 = k[(n - 1) * BLOCK_Q:(n + 1) * BLOCK_Q]
            v2 = v[(n - 1) * BLOCK_Q:(n + 1) * BLOCK_Q]
            mask = band
        for kvh in range(N_KV_HEADS):
            kk = _stacked_halves(k2, kvh)
            vv = _stacked_halves(v2, kvh)
            for cc in range(2):
                c = 2 * kvh + cc
                units.append((_dot_nt(qb[rows, c * LANES:(c + 1) * LANES], kk), vv, mask, c))
    k_prev[...] = k[tile - BLOCK_Q:tile]
    v_prev[...] = v[tile - BLOCK_Q:tile]

    ffn_chunks(1)
    a, u = _gate_nonlin(xc, ra_half, ia_half, b_a_ref[...], b_x_ref[...], lam_ref[...])

    ffn_chunks(1)
    soft = [_sink_softmax_pair(s, mask, sinks_ref[2 * c] * LOG2E, sinks_ref[2 * c + 1] * LOG2E)
            for s, _, mask, c in units]

    outs = [_pv_pair(p, vv, invs) for (p, invs), (_, vv, _, _) in zip(soft, units)]

    ffn_chunks(2)
    h_all, h_last = _scan_groups(a, u, h_carry[...])
    h_carry[...] = h_last

    y_lru = _rms_norm(_gelu_tanh(gate) * h_all, g_lru_ref[...])
    per_block = N_HEADS // 2
    y_attn = jnp.concatenate(
        [jnp.concatenate(outs[n * per_block:(n + 1) * per_block], axis=1) for n in range(n_blocks)], axis=0)
    y_attn = _rms_norm(y_attn, g_attn_ref[...])

    y_mix = (_dot(y_lru.astype(BF16), w_out_ref[0:LRU_WIDTH, :])
             + _dot(y_attn.astype(BF16), w_out_ref[LRU_WIDTH:LRU_WIDTH + Q_DIM, :]) + b_out_ref[...])
    h1 = _layer_norm(ALPHA * x + y_mix, ln1_g_ref[...], ln1_b_ref[...])
    acc_prev = prev_ffn.finish()
    h_prev = h1_scr[...]
    hb = h1.astype(BF16)
    h1_scr[...] = h1
    hb_scr[...] = hb

    head_ffn = _FfnStream(hb, None, wg_ref, wu_ref, wd_ref, 0, FFN_HEAD_CHUNKS)
    head_ffn.emit(1)
    y_ref[...] = _layer_norm(ALPHA * h_prev + acc_prev, ln2_g_ref[...], ln2_b_ref[...])
    acc_scr[...] = head_ffn.finish()

    @pl.when((t_idx == n_t - 1) & (g_idx < n_tiles))
    def _():
        conv_out_ref[...] = xr_ext[hist - (CONV_WIDTH - 1):hist, :]
        lru_out_ref[...] = h_last
        k_out_ref[...] = k[tile - WINDOW:tile]
        v_out_ref[...] = v[tile - WINDOW:tile]


def _full_spec(shape):
    zeros = (0,) * len(shape)
    return pl.BlockSpec(shape, lambda *_: zeros, pipeline_mode=pl.Buffered(1))


def _vec_spec():
    return _full_spec((1, _VEC_LEN))


def _mixer_weight_specs():
    return [
        _vec_spec(), pl.BlockSpec(memory_space=pltpu.SMEM),
        _full_spec((D_MODEL, IN_DIM)),
        _full_spec((2, LRU_WIDTH // 2, LRU_WIDTH // 2)), _full_spec((2, LRU_WIDTH // 2, LRU_WIDTH // 2)),
        _full_spec((LRU_WIDTH + Q_DIM, D_MODEL)),
    ]


def _ffn_weight_specs():
    return [_full_spec((D_MODEL, D_FF)), _full_spec((D_MODEL, D_FF)), _full_spec((D_FF, D_MODEL))]


def _prompt_layer(x, mixer_w, ffn_w):
    bsz, seq, _ = x.shape
    tile = PROMPT_TILE
    assert seq % tile == 0 and tile % BLOCK_Q == 0 and tile >= WINDOW and D_FF % FFN_CHUNK == 0
    n_t = seq // tile
    n_tiles = bsz * n_t
    keep = min(WINDOW, seq)

    def cur(g):
        gi = jnp.minimum(g, n_tiles - 1)
        return lax.div(gi, n_t), lax.rem(gi, n_t)

    def prev(g):
        go = jnp.maximum(g - 1, 0)
        return lax.div(go, n_t), lax.rem(go, n_t)

    in_specs = [pl.BlockSpec((None, tile, D_MODEL), lambda g: (*cur(g), 0)),
                _full_spec((2, tile, LANES)), _full_spec((n_t, 4, LANES))]
    in_specs += _mixer_weight_specs() + _ffn_weight_specs()
    out_shape = (
        jax.ShapeDtypeStruct((bsz, seq, D_MODEL), F32),
        jax.ShapeDtypeStruct((bsz, CONV_WIDTH - 1, LRU_WIDTH), F32),
        jax.ShapeDtypeStruct((bsz, 1, LRU_WIDTH), F32),
        jax.ShapeDtypeStruct((bsz, keep, KV_DIM), F32),
        jax.ShapeDtypeStruct((bsz, keep, KV_DIM), F32),
    )
    out_specs = (
        pl.BlockSpec((None, tile, D_MODEL), lambda g: (*prev(g), 0)),
        pl.BlockSpec((None, CONV_WIDTH - 1, LRU_WIDTH), lambda g: (cur(g)[0], 0, 0)),
        pl.BlockSpec((None, 1, LRU_WIDTH), lambda g: (cur(g)[0], 0, 0)),
        pl.BlockSpec((None, keep, KV_DIM), lambda g: (cur(g)[0], 0, 0)),
        pl.BlockSpec((None, keep, KV_DIM), lambda g: (cur(g)[0], 0, 0)),
    )
    scratch = [
        pltpu.VMEM((tile + SUBLANES, LRU_WIDTH), F32),
        pltpu.VMEM((1, LRU_WIDTH), F32),
        pltpu.VMEM((BLOCK_Q, KV_DIM), F32),
        pltpu.VMEM((BLOCK_Q, KV_DIM), F32),
        pltpu.VMEM((tile, D_MODEL), F32),
        pltpu.VMEM((tile, D_MODEL), BF16),
        pltpu.VMEM((tile, D_MODEL), F32),
    ]
    return pl.pallas_call(
        functools.partial(_prompt_layer_kernel, n_t=n_t, n_tiles=n_tiles),
        grid=(n_tiles + 1,),
        in_specs=in_specs,
        out_specs=out_specs,
        out_shape=out_shape,
        scratch_shapes=scratch,
        compiler_params=pltpu.CompilerParams(
            dimension_semantics=("arbitrary",), vmem_limit_bytes=VMEM_LIMIT_BYTES),
        name="prompt_layer",
    )(x, *_rope_tile_tables(tile, n_t), *mixer_w, *ffn_w)


def _ffn_cast_kernel(h_ref, vec_ref, wg_ref, wu_ref, wd_ref, o_ref, wg_out_ref, wu_out_ref, wd_out_ref,
                     hb_scr, acc_scr):
    c_idx = pl.program_id(0)
    slabs = h_ref.shape[1] // D_MODEL
    rows = h_ref.shape[0]

    @pl.when(c_idx == 0)
    def _():
        for s in range(slabs):
            hb_scr[s * rows:(s + 1) * rows, :] = h_ref[:, s * D_MODEL:(s + 1) * D_MODEL].astype(BF16)
        acc_scr[...] = jnp.zeros_like(acc_scr)

    wg, wu, wd = wg_ref[...].astype(BF16), wu_ref[...].astype(BF16), wd_ref[...].astype(BF16)
    wg_out_ref[...] = wg
    wu_out_ref[...] = wu
    wd_out_ref[...] = wd
    hb = hb_scr[...]
    gt = _dot(hb, wg)
    hid = (gt * jax.nn.sigmoid(gt) * _dot(hb, wu)).astype(BF16)
    acc_scr[...] += _dot(hid, wd)

    @pl.when(c_idx == pl.num_programs(0) - 1)
    def _():
        vec = _vec_views(vec_ref)
        for s in range(slabs):
            h = h_ref[:, s * D_MODEL:(s + 1) * D_MODEL]
            o_ref[:, s * D_MODEL:(s + 1) * D_MODEL] = _layer_norm(
                ALPHA * h + acc_scr[s * rows:(s + 1) * rows, :], vec["ln2_g"][...], vec["ln2_b"][...])


def _ffn_cast(h, vec, wg, wu, wd):
    rows, width = h.shape
    assert width % D_MODEL == 0 and D_FF % FFN_CHUNK == 0
    n_rows = rows * (width // D_MODEL)
    col_chunk = pl.BlockSpec((D_MODEL, FFN_CHUNK), lambda c: (0, c))
    row_chunk = pl.BlockSpec((FFN_CHUNK, D_MODEL), lambda c: (c, 0))
    return pl.pallas_call(
        _ffn_cast_kernel,
        grid=(D_FF // FFN_CHUNK,),
        in_specs=[_full_spec((rows, width)), _vec_spec(), col_chunk, col_chunk, row_chunk],
        out_specs=(pl.BlockSpec((rows, width), lambda c: (0, 0)), col_chunk, col_chunk, row_chunk),
        out_shape=(jax.ShapeDtypeStruct((rows, width), F32),
                   jax.ShapeDtypeStruct((D_MODEL, D_FF), BF16), jax.ShapeDtypeStruct((D_MODEL, D_FF), BF16),
                   jax.ShapeDtypeStruct((D_FF, D_MODEL), BF16)),
        scratch_shapes=[pltpu.VMEM((n_rows, D_MODEL), BF16), pltpu.VMEM((n_rows, D_MODEL), F32)],
        compiler_params=pltpu.CompilerParams(
            dimension_semantics=("arbitrary",), vmem_limit_bytes=VMEM_LIMIT_BYTES),
        name="ffn_cast",
    )(h, vec, wg, wu, wd)


def _decode_mixer_kernel(x_ref, kc_ref, vc_ref, sconv_ref, slru_ref, cos_ref, sin_ref, vec_ref, sinks_ref,
                         w_in_ref, wa_ref, wx_ref, w_out_ref,
                         h1_ref, conv_out_ref, lru_out_ref, k_out_ref, v_out_ref):
    vec = _vec_views(vec_ref)
    b_in_ref, conv_b_ref, b_a_ref, b_x_ref, lam_ref = (vec[n] for n in ("b_in", "conv_b", "b_a", "b_x", "lam"))
    g_lru_ref, g_attn_ref, b_out_ref, ln_g_ref, ln_b_ref = (
        vec[n] for n in ("g_lru", "g_attn", "b_out", "ln1_g", "ln1_b"))
    conv_w_rows = [vec["conv_w%d" % j] for j in range(CONV_WIDTH)]
    nb = x_ref.shape[0]
    steps = x_ref.shape[1] // D_MODEL
    win = kc_ref.shape[1]

    x = jnp.concatenate([x_ref[:, t * D_MODEL:(t + 1) * D_MODEL] for t in range(steps)], axis=0)
    xb = x.astype(BF16)

    def proj(c0, width):
        return _dot(xb, w_in_ref[:, c0:c0 + width]) + b_in_ref[:, c0:c0 + width]

    def step_rows(arr, t):
        return arr[t * nb:(t + 1) * nb]

    xr = proj(_XR0, LRU_WIDTH)
    ext = [sconv_ref[j] for j in range(CONV_WIDTH - 1)]
    ext += [step_rows(xr, t) for t in range(steps)]
    xc_steps = []
    for t in range(steps):
        acc = conv_b_ref[...]
        for j in range(CONV_WIDTH):
            acc = acc + ext[t + j] * conv_w_rows[j][...]
        xc_steps.append(acc)
    for j in range(CONV_WIDTH - 1):
        conv_out_ref[j] = ext[steps + j]
    xc = jnp.concatenate(xc_steps, axis=0)

    ra_half, ia_half = _gate_matmuls(xc, wa_ref, wx_ref)
    a, u = _gate_nonlin(xc, ra_half, ia_half, b_a_ref[...], b_x_ref[...], lam_ref[...])
    h = slru_ref[...]
    hs = []
    for t in range(steps):
        h = step_rows(a, t) * h + step_rows(u, t)
        hs.append(h)
    lru_out_ref[...] = h
    gate = proj(_GATE0, LRU_WIDTH)
    y_lru = _rms_norm(_gelu_tanh(gate) * jnp.concatenate(hs, axis=0), g_lru_ref[...])

    def table(ref):
        return jnp.concatenate([jnp.broadcast_to(ref[t:t + 1, :], (nb, LANES)) for t in range(steps)], axis=0)

    cos, sin = table(cos_ref), table(sin_ref)
    q = _rope(proj(_Q0, Q_DIM), cos * Q_SCALE, sin * Q_SCALE)
    k = _rope(proj(_K0, KV_DIM), cos, sin)
    v = proj(_V0, KV_DIM)

    k_out_ref[:, 0:win - steps, :] = kc_ref[:, steps:win, :]
    v_out_ref[:, 0:win - steps, :] = vc_ref[:, steps:win, :]
    for t in range(steps):
        k_out_ref[:, win - steps + t, :] = step_rows(k, t)
        v_out_ref[:, win - steps + t, :] = step_rows(v, t)

    sub = DEC_SUB
    n_sub = nb // sub
    n_cache = sub * win
    n_keys = n_cache + steps * sub
    qr = lax.broadcasted_iota(jnp.int32, (steps * sub, n_keys), 0)
    kc_col = lax.broadcasted_iota(jnp.int32, (steps * sub, n_keys), 1)
    q_step, q_seq = qr // sub, qr % sub
    cached = kc_col < n_cache
    new_col = kc_col - n_cache
    key_seq = jnp.where(cached, kc_col // win, new_col % sub)
    key_lo = jnp.where(cached, q_step + (win - WINDOW), 0)
    key_hi = jnp.where(cached, win - 1, q_step)
    key_idx = jnp.where(cached, kc_col % win, new_col // sub)
    mask = (key_seq == q_seq) & (key_idx >= key_lo) & (key_idx <= key_hi)

    def sub_rows(arr, j):
        return jnp.concatenate([arr[t * nb + j * sub:t * nb + (j + 1) * sub] for t in range(steps)], axis=0)

    units = []
    for j in range(n_sub):
        qj = sub_rows(q, j).astype(BF16)
        kall = jnp.concatenate([kc_ref[j * sub:(j + 1) * sub].reshape(n_cache, KV_DIM), sub_rows(k, j)], axis=0)
        vall = jnp.concatenate([vc_ref[j * sub:(j + 1) * sub].reshape(n_cache, KV_DIM), sub_rows(v, j)], axis=0)
        for kvh in range(N_KV_HEADS):
            kk = _stacked_halves(kall, kvh)
            vv = _stacked_halves(vall, kvh)
            for cc in range(2):
                c = 2 * kvh + cc
                units.append((_dot_nt(qj[:, c * LANES:(c + 1) * LANES], kk), vv, c))
    soft = [_sink_softmax_pair(s, mask, sinks_ref[2 * c] * LOG2E, sinks_ref[2 * c + 1] * LOG2E) for s, _, c in units]
    outs = [_pv_pair(p, vv, invs) for (p, invs), (_, vv, _) in zip(soft, units)]
    per_sub = N_HEADS // 2
    attn_sub = [jnp.concatenate(outs[j * per_sub:(j + 1) * per_sub], axis=1) for j in range(n_sub)]
    y_attn = jnp.concatenate(
        [attn_sub[j][t * sub:(t + 1) * sub] for t in range(steps) for j in range(n_sub)], axis=0)
    y_attn = _rms_norm(y_attn, g_attn_ref[...])

    y_mix = (_dot(y_lru.astype(BF16), w_out_ref[0:LRU_WIDTH, :])
             + _dot(y_attn.astype(BF16), w_out_ref[LRU_WIDTH:LRU_WIDTH + Q_DIM, :]) + b_out_ref[...])
    h1 = _layer_norm(ALPHA * x + y_mix, ln_g_ref[...], ln_b_ref[...])
    for t in range(steps):
        h1_ref[:, t * D_MODEL:(t + 1) * D_MODEL] = step_rows(h1, t)


def _decode_mixer(x2d, kc, vc, sconv, slru, tables, weights):
    nseq, width = x2d.shape
    steps = width // D_MODEL
    win = kc.shape[1]
    nb = DEC_BLOCK
    assert nseq % nb == 0 and nb % DEC_SUB == 0 and DEC_SUB % SUBLANES == 0 and steps <= win
    conv_spec = pl.BlockSpec((CONV_WIDTH - 1, nb, LRU_WIDTH), lambda i: (0, i, 0))
    in_specs = [
        pl.BlockSpec((nb, width), lambda i: (i, 0)),
        pl.BlockSpec((nb, win, KV_DIM), lambda i: (i, 0, 0)),
        pl.BlockSpec((nb, win, KV_DIM), lambda i: (i, 0, 0)),
        conv_spec,
        pl.BlockSpec((nb, LRU_WIDTH), lambda i: (i, 0)),
    ] + [_full_spec((steps, LANES))] * 2 + _mixer_weight_specs()
    out_shape = (
        jax.ShapeDtypeStruct((nseq, width), F32),
        jax.ShapeDtypeStruct((CONV_WIDTH - 1, nseq, LRU_WIDTH), F32),
        jax.ShapeDtypeStruct((nseq, LRU_WIDTH), F32),
        jax.ShapeDtypeStruct((nseq, win, KV_DIM), F32),
        jax.ShapeDtypeStruct((nseq, win, KV_DIM), F32),
    )
    out_specs = (
        pl.BlockSpec((nb, width), lambda i: (i, 0)),
        conv_spec,
        pl.BlockSpec((nb, LRU_WIDTH), lambda i: (i, 0)),
        pl.BlockSpec((nb, win, KV_DIM), lambda i: (i, 0, 0)),
        pl.BlockSpec((nb, win, KV_DIM), lambda i: (i, 0, 0)),
    )
    return pl.pallas_call(
        _decode_mixer_kernel,
        grid=(nseq // nb,),
        in_specs=in_specs,
        out_specs=out_specs,
        out_shape=out_shape,
        compiler_params=pltpu.CompilerParams(
            dimension_semantics=("arbitrary",), vmem_limit_bytes=VMEM_LIMIT_BYTES),
        name="decode_mixer",
    )(x2d, kc, vc, sconv, slru, *tables, *weights)


def _rope_lanes():
    lane = jnp.arange(LANES, dtype=jnp.int32)
    inv = ROPE_THETA ** (-(lane % HALF_DIM).astype(F32) / HALF_DIM)
    sign = jnp.where(lane % HEAD_DIM < HALF_DIM, -1.0, 1.0).astype(F32)
    return inv, sign


def _rope_tables(positions):
    inv, sign = _rope_lanes()
    ang = positions.astype(F32)[:, None] * inv[None, :]
    return jnp.cos(ang), sign[None, :] * jnp.sin(ang)


def _rope_tile_tables(tile, n_t):
    inv, sign = _rope_lanes()
    off = jnp.arange(tile, dtype=F32)[:, None] * inv[None, :]
    base = (jnp.arange(n_t, dtype=F32) * tile)[:, None] * inv[None, :]
    cos_b, sin_b = jnp.cos(base), jnp.sin(base)
    return (jnp.stack([jnp.cos(off), jnp.sin(off)]),
            jnp.stack([cos_b, sin_b, sign[None, :] * cos_b, sign[None, :] * sin_b], axis=1))


def _block_diag_halves(w):
    per = LRU_BLOCKS // 2
    d = w.shape[-1]
    rows = [jnp.pad(w[h], ((0, 0), ((h % per) * d, (per - 1 - h % per) * d))) for h in range(LRU_BLOCKS)]
    return jnp.stack([jnp.concatenate(rows[:per], axis=0), jnp.concatenate(rows[per:], axis=0)])


def kernel(x_prompt, x_sample, cache_k_win, cache_v_win, state_conv, state_lru, w_in, b_in, conv_w, conv_b,
           w_a, b_a, w_x, b_x, lru_lambda, sinks, g_lru, g_attn, w_out, b_out, ln1_g, ln1_b,
           w_gate, w_up, w_down, ln2_g, ln2_b):
    depth = w_in.shape[0]
    assert depth == 1
    l = 0
    bsz, seq, _ = x_prompt.shape
    nseq, steps, _ = x_sample.shape
    win = cache_k_win.shape[2]

    fields = dict(b_in=b_in, conv_b=conv_b, b_a=b_a, b_x=b_x, lam=lru_lambda, g_lru=g_lru, g_attn=g_attn,
                  b_out=b_out, ln1_g=ln1_g, ln1_b=ln1_b, ln2_g=ln2_g, ln2_b=ln2_b)
    fields.update({"conv_w%d" % j: conv_w[:, j] for j in range(CONV_WIDTH)})
    vec = sum(jnp.pad(fields[name][l].reshape(-1), (off, _VEC_LEN - off - n))
              for name, (off, n) in _VEC_LAYOUT.items()).reshape(1, _VEC_LEN)

    mixer_w = (
        vec, sinks[l], w_in[l].astype(BF16),
        (0.5 * _block_diag_halves(w_a[l])).astype(BF16), (0.5 * _block_diag_halves(w_x[l])).astype(BF16),
        w_out[l].astype(BF16),
    )

    h1_s, conv_s, lru_s, k_s, v_s = _decode_mixer(
        x_sample.reshape(nseq, steps * D_MODEL),
        cache_k_win[l].reshape(nseq, win, KV_DIM), cache_v_win[l].reshape(nseq, win, KV_DIM),
        jnp.transpose(state_conv[l], (1, 0, 2)), state_lru[l],
        _rope_tables(PAST_LEN + jnp.arange(steps, dtype=jnp.int32)), mixer_w)
    y_s, *ffn_w = _ffn_cast(h1_s, vec, w_gate[l], w_up[l], w_down[l])
    y_s = y_s.reshape(nseq, steps, D_MODEL)

    y_p, conv_p, lru_p, k_p, v_p = _prompt_layer(x_prompt, mixer_w, ffn_w)

    keep = k_p.shape[1]
    return (
        y_p, y_s,
        conv_p[None], lru_p.reshape(1, bsz, LRU_WIDTH),
        k_p.reshape(1, bsz, keep, N_KV_HEADS, HEAD_DIM), v_p.reshape(1, bsz, keep, N_KV_HEADS, HEAD_DIM),
        jnp.transpose(conv_s, (1, 0, 2))[None], lru_s[None],
        k_s.reshape(1, nseq, win, N_KV_HEADS, HEAD_DIM), v_s.reshape(1, nseq, win, N_KV_HEADS, HEAD_DIM),
    )
```

```python
import functools

import jax
import jax.numpy as jnp
from jax import lax
from jax.experimental import pallas as pl
from jax.experimental.pallas import tpu as pltpu

D_MODEL = 1024
LRU_WIDTH = 512
LRU_BLOCKS = 8
CONV_WIDTH = 4
LRU_C = 8.0
N_HEADS = 8
N_KV_HEADS = 2
HEAD_DIM = 64
HALF_DIM = HEAD_DIM // 2
Q_DIM = N_HEADS * HEAD_DIM
KV_DIM = N_KV_HEADS * HEAD_DIM
IN_DIM = 2 * LRU_WIDTH + Q_DIM + 2 * KV_DIM
WINDOW = 128
BLOCK_Q = 128
ROPE_THETA = 10000.0
ATTN_SCALE = HEAD_DIM ** -0.5
D_FF = 2816
ALPHA = 2.0 ** 0.25
PAST_LEN = 16384
LOG2E = 1.4426950408889634
Q_SCALE = ATTN_SCALE * LOG2E

LANES = 128
SUBLANES = 8
VMEM_LIMIT_BYTES = 56 * 1024 * 1024

_XR0, _GATE0, _Q0, _K0, _V0 = 0, LRU_WIDTH, 2 * LRU_WIDTH, 2 * LRU_WIDTH + Q_DIM, 2 * LRU_WIDTH + Q_DIM + KV_DIM

PROMPT_TILE = 512
FFN_CHUNK = 256
FFN_HEAD_CHUNKS = 3
DEC_BLOCK = 32
DEC_SUB = 8

BF16 = jnp.bfloat16
F32 = jnp.float32

_VEC_FIELDS = (
    ("b_in", IN_DIM), ("conv_w0", LRU_WIDTH), ("conv_w1", LRU_WIDTH), ("conv_w2", LRU_WIDTH), ("conv_w3", LRU_WIDTH),
    ("conv_b", LRU_WIDTH), ("b_a", LRU_WIDTH), ("b_x", LRU_WIDTH), ("lam", LRU_WIDTH),
    ("g_lru", LRU_WIDTH), ("g_attn", Q_DIM), ("b_out", D_MODEL),
    ("ln1_g", D_MODEL), ("ln1_b", D_MODEL), ("ln2_g", D_MODEL), ("ln2_b", D_MODEL),
)
_VEC_LAYOUT = {}
_VEC_LEN = 0
for _name, _n in _VEC_FIELDS:
    assert _n % 128 == 0
    _VEC_LAYOUT[_name] = (_VEC_LEN, _n)
    _VEC_LEN += _n


def _vec_views(vec_ref):
    return {name: vec_ref.at[:, off:off + n] for name, (off, n) in _VEC_LAYOUT.items()}


def _dot(a, b):
    return jnp.dot(a, b, preferred_element_type=F32)


def _dot_nt(a, b):
    return lax.dot_general(a, b, (((1,), (1,)), ((), ())), preferred_element_type=F32)


def _layer_norm(x, g, b, eps=1e-5):
    mu = jnp.mean(x, axis=-1, keepdims=True)
    xc = x - mu
    var = jnp.mean(xc * xc, axis=-1, keepdims=True)
    return xc * lax.rsqrt(var + eps) * g + b


def _rms_norm(x, g, eps=1e-6):
    return x * lax.rsqrt(jnp.mean(x * x, axis=-1, keepdims=True) + eps) * g


def _gelu_tanh(x):
    c = 0.7978845608028654
    return 0.5 * x * (1.0 + jnp.tanh(c * (x + 0.044715 * (x * x * x))))


def _softplus(x):
    return jnp.maximum(x, 0.0) + jnp.log1p(jnp.exp(-jnp.abs(x)))


def _swap_halves(x):
    w = x.shape[-1]
    lane = lax.broadcasted_iota(jnp.int32, (1, w), 1)
    first_half = (lane % HEAD_DIM) < HALF_DIM
    return jnp.where(first_half, pltpu.roll(x, w - HALF_DIM, 1), pltpu.roll(x, HALF_DIM, 1))


def _rope(x, cos, sin_signed):
    reps = x.shape[-1] // LANES
    if reps > 1:
        cos = jnp.concatenate([cos] * reps, axis=1)
        sin_signed = jnp.concatenate([sin_signed] * reps, axis=1)
    return x * cos + _swap_halves(x) * sin_signed


def _gate_matmuls(xc, wa_ref, wx_ref):
    xcb = xc.astype(BF16)
    half = LRU_WIDTH // 2
    lo, hi = xcb[:, :half], xcb[:, half:]
    ra = jnp.concatenate([_dot(lo, wa_ref[0]), _dot(hi, wa_ref[1])], axis=1)
    ia = jnp.concatenate([_dot(lo, wx_ref[0]), _dot(hi, wx_ref[1])], axis=1)
    return ra, ia


def _gate_nonlin(xc, ra_half, ia_half, b_a, b_x, lam):
    c_half = (-0.5 * LRU_C) * _softplus(-lam)
    t_r = jnp.tanh(ra_half + 0.5 * b_a)
    neg_log_a = (-c_half) * t_r - c_half
    a = jnp.exp2(neg_log_a * (-LOG2E))
    y = jnp.tanh(neg_log_a) * (a * a + 1.0)
    mult = jnp.where(y > 0.0, y * lax.rsqrt(y), 0.0)
    t_i = jnp.tanh(ia_half + 0.5 * b_x)
    hx = 0.5 * xc
    u = mult * (hx * t_i + hx)
    return a, u


def _lane_halves(x, kvh):
    lane = lax.broadcasted_iota(jnp.int32, (1, x.shape[1]), 1)
    low = lane < HEAD_DIM
    rolled = pltpu.roll(x, HEAD_DIM, 1)
    if kvh == 0:
        return jnp.where(low, x, 0.0), jnp.where(low, 0.0, rolled)
    return jnp.where(low, rolled, 0.0), jnp.where(low, 0.0, x)


def _stacked_halves(x, kvh):
    return jnp.concatenate(_lane_halves(x, kvh), axis=0).astype(BF16)


def _sink_softmax_pair(scores, mask, sink_lo, sink_hi):
    s_len = mask.shape[1]
    ps, invs = [], []
    for half, sink in ((0, sink_lo), (1, sink_hi)):
        sh = jnp.where(mask, scores[:, half * s_len:(half + 1) * s_len], -jnp.inf)
        m = jnp.maximum(jnp.max(sh, axis=-1, keepdims=True), sink)
        p = jnp.exp2(sh - m)
        denom = jnp.sum(p, axis=-1, keepdims=True) + jnp.exp2(sink - m)
        ps.append(p.astype(BF16))
        invs.append(1.0 / denom)
    return jnp.concatenate(ps, axis=1), invs


def _pv_pair(probs, vv, invs):
    out = _dot(probs, vv)
    lane = lax.broadcasted_iota(jnp.int32, (1, LANES), 1)
    return out * jnp.where(lane < HEAD_DIM, invs[0], invs[1])


def _scan_groups(a, u, h0):
    rows, width = a.shape
    groups = rows // SUBLANES
    a3 = a.reshape(groups, SUBLANES, width)
    u3 = u.reshape(groups, SUBLANES, width)
    row = lax.broadcasted_iota(jnp.int32, (1, SUBLANES, 1), 1)
    d = 1
    while d < SUBLANES:
        valid = row >= d
        u3 = u3 + a3 * jnp.where(valid, pltpu.roll(u3, d, 1), 0.0)
        a3 = a3 * jnp.where(valid, pltpu.roll(a3, d, 1), 1.0)
        d *= 2
    h = h0
    hs = []
    for g in range(groups):
        hg = u3[g] + a3[g] * h
        h = hg[SUBLANES - 1:SUBLANES, :]
        hs.append(hg)
    return jnp.concatenate(hs, axis=0), h


class _FfnStream:
    def __init__(self, lhs, acc, wg_ref, wu_ref, wd_ref, first, last):
        self.lhs, self.acc = lhs, acc
        self.wg_ref, self.wu_ref, self.wd_ref = wg_ref, wu_ref, wd_ref
        self.next, self.last = first, last
        self.pending = []

    def _drain(self):
        gt, up, c = self.pending.pop(0)
        hid = (gt * jax.nn.sigmoid(gt) * up).astype(BF16)
        part = _dot(hid, self.wd_ref[c * FFN_CHUNK:(c + 1) * FFN_CHUNK, :])
        self.acc = part if self.acc is None else self.acc + part

    def emit(self, count):
        for _ in range(count):
            if self.next < self.last:
                cols = slice(self.next * FFN_CHUNK, (self.next + 1) * FFN_CHUNK)
                self.pending.append((_dot(self.lhs, self.wg_ref[:, cols]), _dot(self.lhs, self.wu_ref[:, cols]),
                                     self.next))
                self.next += 1
                if len(self.pending) > 1:
                    self._drain()

    def finish(self):
        self.emit(self.last - self.next)
        while self.pending:
            self._drain()
        return self.acc


def _prompt_layer_kernel(x_ref, rope_off_ref, rope_base_ref, vec_ref, sinks_ref,
                         w_in_ref, wa_ref, wx_ref, w_out_ref, wg_ref, wu_ref, wd_ref,
                         y_ref, conv_out_ref, lru_out_ref, k_out_ref, v_out_ref,
                         xr_ext, h_carry, k_prev, v_prev, h1_scr, hb_scr, acc_scr, *, n_t, n_tiles):
    vec = _vec_views(vec_ref)
    b_in_ref, conv_b_ref, b_a_ref, b_x_ref, lam_ref = (vec[n] for n in ("b_in", "conv_b", "b_a", "b_x", "lam"))
    g_lru_ref, g_attn_ref, b_out_ref = (vec[n] for n in ("g_lru", "g_attn", "b_out"))
    ln1_g_ref, ln1_b_ref, ln2_g_ref, ln2_b_ref = (vec[n] for n in ("ln1_g", "ln1_b", "ln2_g", "ln2_b"))
    conv_w_rows = [vec["conv_w%d" % j] for j in range(CONV_WIDTH)]
    g_idx = pl.program_id(0)
    t_idx = lax.rem(g_idx, n_t)
    tile = x_ref.shape[0]
    hist = SUBLANES

    @pl.when(g_idx == 0)
    def _():
        h1_scr[...] = jnp.zeros_like(h1_scr)
        hb_scr[...] = jnp.zeros_like(hb_scr)
        acc_scr[...] = jnp.zeros_like(acc_scr)

    @pl.when(t_idx == 0)
    def _():
        xr_ext[0:hist, :] = jnp.zeros((hist, LRU_WIDTH), F32)
        h_carry[...] = jnp.zeros_like(h_carry)
        k_prev[...] = jnp.zeros_like(k_prev)
        v_prev[...] = jnp.zeros_like(v_prev)

    n_chunks = D_FF // FFN_CHUNK
    prev_ffn = _FfnStream(hb_scr[...], acc_scr[...], wg_ref, wu_ref, wd_ref, FFN_HEAD_CHUNKS, n_chunks)
    ffn_chunks = prev_ffn.emit

    x = x_ref[...]
    xb = x.astype(BF16)

    def proj(c0, width):
        return _dot(xb, w_in_ref[:, c0:c0 + width]) + b_in_ref[:, c0:c0 + width]

    xr = proj(_XR0, LRU_WIDTH)
    xr_ext[hist:hist + tile, :] = xr
    q_raw = proj(_Q0, Q_DIM)
    k_raw = proj(_K0, KV_DIM)
    v = proj(_V0, KV_DIM)
    gate = proj(_GATE0, LRU_WIDTH)

    ffn_chunks(1)
    xc = conv_b_ref[...] + xr * conv_w_rows[CONV_WIDTH - 1][...]
    for j in range(CONV_WIDTH - 1):
        shift = CONV_WIDTH - 1 - j
        xc = xc + xr_ext[hist - shift:hist - shift + tile, :] * conv_w_rows[j][...]
    xr_ext[0:hist, :] = xr_ext[tile:tile + hist, :]

    ra_half, ia_half = _gate_matmuls(xc, wa_ref, wx_ref)

    ffn_chunks(1)
    cos_o, sin_o = rope_off_ref[0], rope_off_ref[1]
    base = rope_base_ref[pl.ds(t_idx, 1)][0]
    cos = base[0:1] * cos_o - base[1:2] * sin_o
    sin = base[3:4] * cos_o + base[2:3] * sin_o
    qb = _rope(q_raw, cos * Q_SCALE, sin * Q_SCALE).astype(BF16)
    k = _rope(k_raw, cos, sin)

    qi = lax.broadcasted_iota(jnp.int32, (BLOCK_Q, 2 * BLOCK_Q), 0)
    kj = lax.broadcasted_iota(jnp.int32, (BLOCK_Q, 2 * BLOCK_Q), 1)
    band = (kj >= qi) & (kj <= qi + WINDOW)
    first_lo = jnp.where(t_idx == 0, BLOCK_Q, 0)
    band_first = band & (kj >= first_lo)

    n_blocks = tile // BLOCK_Q
    units = []
    for n in range(n_blocks):
        rows = slice(n * BLOCK_Q, (n + 1) * BLOCK_Q)
        if n == 0:
            k2 = jnp.concatenate([k_prev[...], k[rows]], axis=0)
            v2 = jnp.concatenate([v_prev[...], v[rows]], axis=0)
            mask = band_first
        else:
            k2 = k[(n - 1) * BLOCK_Q:(n + 1) * BLOCK_Q]
            v2 = v[(n - 1) * BLOCK_Q:(n + 1) * BLOCK_Q]
            mask = band
        for kvh in range(N_KV_HEADS):
            kk = _stacked_halves(k2, kvh)
            vv = _stacked_halves(v2, kvh)
            for cc in range(2):
                c = 2 * kvh + cc
                units.append((_dot_nt(qb[rows, c * LANES:(c + 1) * LANES], kk), vv, mask, c))
    k_prev[...] = k[tile - BLOCK_Q:tile]
    v_prev[...] = v[tile - BLOCK_Q:tile]

    ffn_chunks(1)
    a, u = _gate_nonlin(xc, ra_half, ia_half, b_a_ref[...], b_x_ref[...], lam_ref[...])

    ffn_chunks(1)
    soft = [_sink_softmax_pair(s, mask, sinks_ref[2 * c] * LOG2E, sinks_ref[2 * c + 1] * LOG2E)
            for s, _, mask, c in units]

    outs = [_pv_pair(p, vv, invs) for (p, invs), (_, vv, _, _) in zip(soft, units)]

    ffn_chunks(2)
    h_all, h_last = _scan_groups(a, u, h_carry[...])
    h_carry[...] = h_last

    y_lru = _rms_norm(_gelu_tanh(gate) * h_all, g_lru_ref[...])
    per_block = N_HEADS // 2
    y_attn = jnp.concatenate(
        [jnp.concatenate(outs[n * per_block:(n + 1) * per_block], axis=1) for n in range(n_blocks)], axis=0)
    y_attn = _rms_norm(y_attn, g_attn_ref[...])

    y_mix = (_dot(y_lru.astype(BF16), w_out_ref[0:LRU_WIDTH, :])
             + _dot(y_attn.astype(BF16), w_out_ref[LRU_WIDTH:LRU_WIDTH + Q_DIM, :]) + b_out_ref[...])
    h1 = _layer_norm(ALPHA * x + y_mix, ln1_g_ref[...], ln1_b_ref[...])
    acc_prev = prev_ffn.finish()
    h_prev = h1_scr[...]
    hb = h1.astype(BF16)
    h1_scr[...] = h1
    hb_scr[...] = hb

    head_ffn = _FfnStream(hb, None, wg_ref, wu_ref, wd_ref, 0, FFN_HEAD_CHUNKS)
    head_ffn.emit(1)
    y_ref[...] = _layer_norm(ALPHA * h_prev + acc_prev, ln2_g_ref[...], ln2_b_ref[...])
    acc_scr[...] = head_ffn.finish()

    @pl.when((t_idx == n_t - 1) & (g_idx < n_tiles))
    def _():
        conv_out_ref[...] = xr_ext[hist - (CONV_WIDTH - 1):hist, :]
        lru_out_ref[...] = h_last
        k_out_ref[...] = k[tile - WINDOW:tile].T
        v_out_ref[...] = v[tile - WINDOW:tile].T


def _full_spec(shape):
    zeros = (0,) * len(shape)
    return pl.BlockSpec(shape, lambda *_: zeros, pipeline_mode=pl.Buffered(1))


def _vec_spec():
    return _full_spec((1, _VEC_LEN))


def _mixer_weight_specs():
    return [
        _vec_spec(), pl.BlockSpec(memory_space=pltpu.SMEM),
        _full_spec((D_MODEL, IN_DIM)),
        _full_spec((2, LRU_WIDTH // 2, LRU_WIDTH // 2)), _full_spec((2, LRU_WIDTH // 2, LRU_WIDTH // 2)),
        _full_spec((LRU_WIDTH + Q_DIM, D_MODEL)),
    ]


def _ffn_weight_specs():
    return [_full_spec((D_MODEL, D_FF)), _full_spec((D_MODEL, D_FF)), _full_spec((D_FF, D_MODEL))]


def _prompt_layer(x, mixer_w, ffn_w):
    bsz, seq, _ = x.shape
    tile = PROMPT_TILE
    assert seq % tile == 0 and tile % BLOCK_Q == 0 and tile >= WINDOW and D_FF % FFN_CHUNK == 0
    n_t = seq // tile
    n_tiles = bsz * n_t
    keep = min(WINDOW, seq)

    def cur(g):
        gi = jnp.minimum(g, n_tiles - 1)
        return lax.div(gi, n_t), lax.rem(gi, n_t)

    def prev(g):
        go = jnp.maximum(g - 1, 0)
        return lax.div(go, n_t), lax.rem(go, n_t)

    in_specs = [pl.BlockSpec((None, tile, D_MODEL), lambda g: (*cur(g), 0)),
                _full_spec((2, tile, LANES)), _full_spec((n_t, 4, LANES))]
    in_specs += _mixer_weight_specs() + _ffn_weight_specs()
    out_shape = (
        jax.ShapeDtypeStruct((bsz, seq, D_MODEL), F32),
        jax.ShapeDtypeStruct((bsz, CONV_WIDTH - 1, LRU_WIDTH), F32),
        jax.ShapeDtypeStruct((bsz, 1, LRU_WIDTH), F32),
        jax.ShapeDtypeStruct((bsz, KV_DIM, keep), F32),
        jax.ShapeDtypeStruct((bsz, KV_DIM, keep), F32),
    )
    out_specs = (
        pl.BlockSpec((None, tile, D_MODEL), lambda g: (*prev(g), 0)),
        pl.BlockSpec((None, CONV_WIDTH - 1, LRU_WIDTH), lambda g: (cur(g)[0], 0, 0)),
        pl.BlockSpec((None, 1, LRU_WIDTH), lambda g: (cur(g)[0], 0, 0)),
        pl.BlockSpec((None, KV_DIM, keep), lambda g: (cur(g)[0], 0, 0)),
        pl.BlockSpec((None, KV_DIM, keep), lambda g: (cur(g)[0], 0, 0)),
    )
    scratch = [
        pltpu.VMEM((tile + SUBLANES, LRU_WIDTH), F32),
        pltpu.VMEM((1, LRU_WIDTH), F32),
        pltpu.VMEM((BLOCK_Q, KV_DIM), F32),
        pltpu.VMEM((BLOCK_Q, KV_DIM), F32),
        pltpu.VMEM((tile, D_MODEL), F32),
        pltpu.VMEM((tile, D_MODEL), BF16),
        pltpu.VMEM((tile, D_MODEL), F32),
    ]
    return pl.pallas_call(
        functools.partial(_prompt_layer_kernel, n_t=n_t, n_tiles=n_tiles),
        grid=(n_tiles + 1,),
        in_specs=in_specs,
        out_specs=out_specs,
        out_shape=out_shape,
        scratch_shapes=scratch,
        compiler_params=pltpu.CompilerParams(
            dimension_semantics=("arbitrary",), vmem_limit_bytes=VMEM_LIMIT_BYTES),
        name="prompt_layer",
    )(x, *_rope_tile_tables(tile, n_t), *mixer_w, *ffn_w)


def _ffn_cast_kernel(h_ref, vec_ref, wg_ref, wu_ref, wd_ref, o_ref, wg_out_ref, wu_out_ref, wd_out_ref,
                     hb_scr, acc_scr):
    c_idx = pl.program_id(0)
    slabs = h_ref.shape[1] // D_MODEL
    rows = h_ref.shape[0]

    @pl.when(c_idx == 0)
    def _():
        for s in range(slabs):
            hb_scr[s * rows:(s + 1) * rows, :] = h_ref[:, s * D_MODEL:(s + 1) * D_MODEL].astype(BF16)
        acc_scr[...] = jnp.zeros_like(acc_scr)

    wg, wu, wd = wg_ref[...].astype(BF16), wu_ref[...].astype(BF16), wd_ref[...].astype(BF16)
    wg_out_ref[...] = wg
    wu_out_ref[...] = wu
    wd_out_ref[...] = wd
    hb = hb_scr[...]
    gt = _dot(hb, wg)
    hid = (gt * jax.nn.sigmoid(gt) * _dot(hb, wu)).astype(BF16)
    acc_scr[...] += _dot(hid, wd)

    @pl.when(c_idx == pl.num_programs(0) - 1)
    def _():
        vec = _vec_views(vec_ref)
        for s in range(slabs):
            h = h_ref[:, s * D_MODEL:(s + 1) * D_MODEL]
            o_ref[:, s * D_MODEL:(s + 1) * D_MODEL] = _layer_norm(
                ALPHA * h + acc_scr[s * rows:(s + 1) * rows, :], vec["ln2_g"][...], vec["ln2_b"][...])


def _ffn_cast(h, vec, wg, wu, wd):
    rows, width = h.shape
    assert width % D_MODEL == 0 and D_FF % FFN_CHUNK == 0
    n_rows = rows * (width // D_MODEL)
    col_chunk = pl.BlockSpec((D_MODEL, FFN_CHUNK), lambda c: (0, c))
    row_chunk = pl.BlockSpec((FFN_CHUNK, D_MODEL), lambda c: (c, 0))
    return pl.pallas_call(
        _ffn_cast_kernel,
        grid=(D_FF // FFN_CHUNK,),
        in_specs=[_full_spec((rows, width)), _vec_spec(), col_chunk, col_chunk, row_chunk],
        out_specs=(pl.BlockSpec((rows, width), lambda c: (0, 0)), col_chunk, col_chunk, row_chunk),
        out_shape=(jax.ShapeDtypeStruct((rows, width), F32),
                   jax.ShapeDtypeStruct((D_MODEL, D_FF), BF16), jax.ShapeDtypeStruct((D_MODEL, D_FF), BF16),
                   jax.ShapeDtypeStruct((D_FF, D_MODEL), BF16)),
        scratch_shapes=[pltpu.VMEM((n_rows, D_MODEL), BF16), pltpu.VMEM((n_rows, D_MODEL), F32)],
        compiler_params=pltpu.CompilerParams(
            dimension_semantics=("arbitrary",), vmem_limit_bytes=VMEM_LIMIT_BYTES),
        name="ffn_cast",
    )(h, vec, wg, wu, wd)


def _decode_mixer_kernel(x_ref, kc_ref, vc_ref, sconv_ref, slru_ref, cos_ref, sin_ref, vec_ref, sinks_ref,
                         w_in_ref, wa_ref, wx_ref, w_out_ref,
                         h1_ref, conv_out_ref, lru_out_ref, k_out_ref, v_out_ref):
    vec = _vec_views(vec_ref)
    b_in_ref, conv_b_ref, b_a_ref, b_x_ref, lam_ref = (vec[n] for n in ("b_in", "conv_b", "b_a", "b_x", "lam"))
    g_lru_ref, g_attn_ref, b_out_ref, ln_g_ref, ln_b_ref = (
        vec[n] for n in ("g_lru", "g_attn", "b_out", "ln1_g", "ln1_b"))
    conv_w_rows = [vec["conv_w%d" % j] for j in range(CONV_WIDTH)]
    nb = x_ref.shape[0]
    steps = x_ref.shape[1] // D_MODEL
    win = kc_ref.shape[2]
    kc_val = jnp.transpose(kc_ref[...], (0, 2, 1))
    vc_val = jnp.transpose(vc_ref[...], (0, 2, 1))

    x = jnp.concatenate([x_ref[:, t * D_MODEL:(t + 1) * D_MODEL] for t in range(steps)], axis=0)
    xb = x.astype(BF16)

    def proj(c0, width):
        return _dot(xb, w_in_ref[:, c0:c0 + width]) + b_in_ref[:, c0:c0 + width]

    def step_rows(arr, t):
        return arr[t * nb:(t + 1) * nb]

    xr = proj(_XR0, LRU_WIDTH)
    ext = [sconv_ref[j] for j in range(CONV_WIDTH - 1)]
    ext += [step_rows(xr, t) for t in range(steps)]
    xc_steps = []
    for t in range(steps):
        acc = conv_b_ref[...]
        for j in range(CONV_WIDTH):
            acc = acc + ext[t + j] * conv_w_rows[j][...]
        xc_steps.append(acc)
    for j in range(CONV_WIDTH - 1):
        conv_out_ref[j] = ext[steps + j]
    xc = jnp.concatenate(xc_steps, axis=0)

    ra_half, ia_half = _gate_matmuls(xc, wa_ref, wx_ref)
    a, u = _gate_nonlin(xc, ra_half, ia_half, b_a_ref[...], b_x_ref[...], lam_ref[...])
    h = slru_ref[...]
    hs = []
    for t in range(steps):
        h = step_rows(a, t) * h + step_rows(u, t)
        hs.append(h)
    lru_out_ref[...] = h
    gate = proj(_GATE0, LRU_WIDTH)
    y_lru = _rms_norm(_gelu_tanh(gate) * jnp.concatenate(hs, axis=0), g_lru_ref[...])

    def table(ref):
        return jnp.concatenate([jnp.broadcast_to(ref[t:t + 1, :], (nb, LANES)) for t in range(steps)], axis=0)

    cos, sin = table(cos_ref), table(sin_ref)
    q = _rope(proj(_Q0, Q_DIM), cos * Q_SCALE, sin * Q_SCALE)
    k = _rope(proj(_K0, KV_DIM), cos, sin)
    v = proj(_V0, KV_DIM)

    k_out_ref[:, 0:win - steps, :] = kc_val[:, steps:win, :]
    v_out_ref[:, 0:win - steps, :] = vc_val[:, steps:win, :]
    for t in range(steps):
        k_out_ref[:, win - steps + t, :] = step_rows(k, t)
        v_out_ref[:, win - steps + t, :] = step_rows(v, t)

    sub = DEC_SUB
    n_sub = nb // sub
    n_cache = sub * win
    n_keys = n_cache + steps * sub
    qr = lax.broadcasted_iota(jnp.int32, (steps * sub, n_keys), 0)
    kc_col = lax.broadcasted_iota(jnp.int32, (steps * sub, n_keys), 1)
    q_step, q_seq = qr // sub, qr % sub
    cached = kc_col < n_cache
    new_col = kc_col - n_cache
    key_seq = jnp.where(cached, kc_col // win, new_col % sub)
    key_lo = jnp.where(cached, q_step + (win - WINDOW), 0)
    key_hi = jnp.where(cached, win - 1, q_step)
    key_idx = jnp.where(cached, kc_col % win, new_col // sub)
    mask = (key_seq == q_seq) & (key_idx >= key_lo) & (key_idx <= key_hi)

    def sub_rows(arr, j):
        return jnp.concatenate([arr[t * nb + j * sub:t * nb + (j + 1) * sub] for t in range(steps)], axis=0)

    units = []
    for j in range(n_sub):
        qj = sub_rows(q, j).astype(BF16)
        kall = jnp.concatenate([kc_val[j * sub:(j + 1) * sub].reshape(n_cache, KV_DIM), sub_rows(k, j)], axis=0)
        vall = jnp.concatenate([vc_val[j * sub:(j + 1) * sub].reshape(n_cache, KV_DIM), sub_rows(v, j)], axis=0)
        for kvh in range(N_KV_HEADS):
            kk = _stacked_halves(kall, kvh)
            vv = _stacked_halves(vall, kvh)
            for cc in range(2):
                c = 2 * kvh + cc
                units.append((_dot_nt(qj[:, c * LANES:(c + 1) * LANES], kk), vv, c))
    soft = [_sink_softmax_pair(s, mask, sinks_ref[2 * c] * LOG2E, sinks_ref[2 * c + 1] * LOG2E) for s, _, c in units]
    outs = [_pv_pair(p, vv, invs) for (p, invs), (_, vv, _) in zip(soft, units)]
    per_sub = N_HEADS // 2
    attn_sub = [jnp.concatenate(outs[j * per_sub:(j + 1) * per_sub], axis=1) for j in range(n_sub)]
    y_attn = jnp.concatenate(
        [attn_sub[j][t * sub:(t + 1) * sub] for t in range(steps) for j in range(n_sub)], axis=0)
    y_attn = _rms_norm(y_attn, g_attn_ref[...])

    y_mix = (_dot(y_lru.astype(BF16), w_out_ref[0:LRU_WIDTH, :])
             + _dot(y_attn.astype(BF16), w_out_ref[LRU_WIDTH:LRU_WIDTH + Q_DIM, :]) + b_out_ref[...])
    h1 = _layer_norm(ALPHA * x + y_mix, ln_g_ref[...], ln_b_ref[...])
    for t in range(steps):
        h1_ref[:, t * D_MODEL:(t + 1) * D_MODEL] = step_rows(h1, t)


def _decode_mixer(x2d, kc, vc, sconv, slru, tables, weights):
    nseq, width = x2d.shape
    steps = width // D_MODEL
    win = kc.shape[2]
    nb = DEC_BLOCK
    assert nseq % nb == 0 and nb % DEC_SUB == 0 and DEC_SUB % SUBLANES == 0 and steps <= win
    conv_spec = pl.BlockSpec((CONV_WIDTH - 1, nb, LRU_WIDTH), lambda i: (0, i, 0))
    in_specs = [
        pl.BlockSpec((nb, width), lambda i: (i, 0)),
        pl.BlockSpec((nb, KV_DIM, win), lambda i: (i, 0, 0)),
        pl.BlockSpec((nb, KV_DIM, win), lambda i: (i, 0, 0)),
        conv_spec,
        pl.BlockSpec((nb, LRU_WIDTH), lambda i: (i, 0)),
    ] + [_full_spec((steps, LANES))] * 2 + _mixer_weight_specs()
    out_shape = (
        jax.ShapeDtypeStruct((nseq, width), F32),
        jax.ShapeDtypeStruct((CONV_WIDTH - 1, nseq, LRU_WIDTH), F32),
        jax.ShapeDtypeStruct((nseq, LRU_WIDTH), F32),
        jax.ShapeDtypeStruct((nseq, win, KV_DIM), F32),
        jax.ShapeDtypeStruct((nseq, win, KV_DIM), F32),
    )
    out_specs = (
        pl.BlockSpec((nb, width), lambda i: (i, 0)),
        conv_spec,
        pl.BlockSpec((nb, LRU_WIDTH), lambda i: (i, 0)),
        pl.BlockSpec((nb, win, KV_DIM), lambda i: (i, 0, 0)),
        pl.BlockSpec((nb, win, KV_DIM), lambda i: (i, 0, 0)),
    )
    return pl.pallas_call(
        _decode_mixer_kernel,
        grid=(nseq // nb,),
        in_specs=in_specs,
        out_specs=out_specs,
        out_shape=out_shape,
        compiler_params=pltpu.CompilerParams(
            dimension_semantics=("arbitrary",), vmem_limit_bytes=VMEM_LIMIT_BYTES),
        name="decode_mixer",
    )(x2d, kc, vc, sconv, slru, *tables, *weights)


def _rope_lanes():
    lane = jnp.arange(LANES, dtype=jnp.int32)
    inv = ROPE_THETA ** (-(lane % HALF_DIM).astype(F32) / HALF_DIM)
    sign = jnp.where(lane % HEAD_DIM < HALF_DIM, -1.0, 1.0).astype(F32)
    return inv, sign


def _rope_tables(positions):
    inv, sign = _rope_lanes()
    ang = positions.astype(F32)[:, None] * inv[None, :]
    return jnp.cos(ang), sign[None, :] * jnp.sin(ang)


def _rope_tile_tables(tile, n_t):
    inv, sign = _rope_lanes()
    off = jnp.arange(tile, dtype=F32)[:, None] * inv[None, :]
    base = (jnp.arange(n_t, dtype=F32) * tile)[:, None] * inv[None, :]
    cos_b, sin_b = jnp.cos(base), jnp.sin(base)
    return (jnp.stack([jnp.cos(off), jnp.sin(off)]),
            jnp.stack([cos_b, sin_b, sign[None, :] * cos_b, sign[None, :] * sin_b], axis=1))


def _window_to_stored(win_arr):
    nseq, win = win_arr.shape[:2]
    return jnp.transpose(win_arr, (0, 2, 3, 1)).reshape(nseq, KV_DIM, win)


def _window_from_stored(stored):
    nseq, _, win = stored.shape
    return jnp.transpose(stored.reshape(nseq, N_KV_HEADS, HEAD_DIM, win), (0, 3, 1, 2))


def _block_diag_halves(w):
    per = LRU_BLOCKS // 2
    d = w.shape[-1]
    rows = [jnp.pad(w[h], ((0, 0), ((h % per) * d, (per - 1 - h % per) * d))) for h in range(LRU_BLOCKS)]
    return jnp.stack([jnp.concatenate(rows[:per], axis=0), jnp.concatenate(rows[per:], axis=0)])


def kernel(x_prompt, x_sample, cache_k_win, cache_v_win, state_conv, state_lru, w_in, b_in, conv_w, conv_b,
           w_a, b_a, w_x, b_x, lru_lambda, sinks, g_lru, g_attn, w_out, b_out, ln1_g, ln1_b,
           w_gate, w_up, w_down, ln2_g, ln2_b):
    depth = w_in.shape[0]
    assert depth == 1
    l = 0
    bsz, seq, _ = x_prompt.shape
    nseq, steps, _ = x_sample.shape
    win = cache_k_win.shape[2]

    fields = dict(b_in=b_in, conv_b=conv_b, b_a=b_a, b_x=b_x, lam=lru_lambda, g_lru=g_lru, g_attn=g_attn,
                  b_out=b_out, ln1_g=ln1_g, ln1_b=ln1_b, ln2_g=ln2_g, ln2_b=ln2_b)
    fields.update({"conv_w%d" % j: conv_w[:, j] for j in range(CONV_WIDTH)})
    vec = sum(jnp.pad(fields[name][l].reshape(-1), (off, _VEC_LEN - off - n))
              for name, (off, n) in _VEC_LAYOUT.items()).reshape(1, _VEC_LEN)

    mixer_w = (
        vec, sinks[l], w_in[l].astype(BF16),
        (0.5 * _block_diag_halves(w_a[l])).astype(BF16), (0.5 * _block_diag_halves(w_x[l])).astype(BF16),
        w_out[l].astype(BF16),
    )

    h1_s, conv_s, lru_s, k_s, v_s = _decode_mixer(
        x_sample.reshape(nseq, steps * D_MODEL),
        _window_to_stored(cache_k_win[l]), _window_to_stored(cache_v_win[l]),
        jnp.transpose(state_conv[l], (1, 0, 2)), state_lru[l],
        _rope_tables(PAST_LEN + jnp.arange(steps, dtype=jnp.int32)), mixer_w)
    y_s, *ffn_w = _ffn_cast(h1_s, vec, w_gate[l], w_up[l], w_down[l])
    y_s = y_s.reshape(nseq, steps, D_MODEL)

    y_p, conv_p, lru_p, k_p, v_p = _prompt_layer(x_prompt, mixer_w, ffn_w)

    return (
        y_p, y_s,
        conv_p[None], lru_p.reshape(1, bsz, LRU_WIDTH),
        _window_from_stored(k_p)[None], _window_from_stored(v_p)[None],
        jnp.transpose(conv_s, (1, 0, 2))[None], lru_s[None],
        k_s.reshape(1, nseq, win, N_KV_HEADS, HEAD_DIM), v_s.reshape(1, nseq, win, N_KV_HEADS, HEAD_DIM),
    )
```
